```python
import math
import jax, jax.numpy as jnp
from jax import lax
import numpy as np

D_MODEL = 1024
BATCH = 8
SEQ = 4096
DEPTH = 1
DEC_BATCH = 32
DEC_SEQ = 8
PAST_LEN = 16384
PAGE_SIZE = 128

D_MIX = D_MODEL
D_A = D_MIX // 2
G_A = 4
CH_A = D_A // G_A
CHUNK = 128
D_B = D_MIX - D_A
H_B = 4
DV = D_B // H_B
DH = DV // 2
D_QK = H_B * 2 * DH
D_IN = 2 * D_A + 2 * D_QK + D_B
SCALE = DH ** -0.5
LAYER_INDEX = 1
LAMBDA_INIT = 0.8 - 0.6 * math.exp(-0.3 * (LAYER_INDEX - 1))
NUM_BUCKETS = 32
MAX_EXACT = NUM_BUCKETS // 2
MAX_DISTANCE = 128
Q_BLOCK = 128
D_FF = ((int(math.ceil(8 * D_MODEL / 3)) + 255) // 256) * 256
EPS = 1e-6

kernel_name = "hybrid_gmlp_diffattn_decode_step"


def rmsnorm(x, g):
    x32 = x.astype(jnp.float32)
    y = x32 * lax.rsqrt(jnp.mean(x32 * x32, axis=-1, keepdims=True) + EPS)
    return (y * g.astype(jnp.float32)).astype(x.dtype)


def split_proj(xn, w_in):
    z = jnp.einsum('bsd,de->bse', xn, w_in)
    u, v, q, k, vb = jnp.split(z, [D_A, 2 * D_A, 2 * D_A + D_QK, 2 * D_A + 2 * D_QK], axis=-1)
    b, s = xn.shape[0], xn.shape[1]
    u = u.reshape(b, s, G_A, CH_A)
    v = v.reshape(b, s, G_A, CH_A)
    q = q.reshape(b, s, H_B, 2 * DH)
    k = k.reshape(b, s, H_B, 2 * DH)
    vb = vb.reshape(b, s, H_B, DV)
    return u, v, q, k, vb


def group_layernorm(v, g, b):
    v32 = v.astype(jnp.float32)
    mu = jnp.mean(v32, axis=-1, keepdims=True)
    var = jnp.mean(jnp.square(v32 - mu), axis=-1, keepdims=True)
    y = (v32 - mu) * lax.rsqrt(var + EPS)
    y = y * g.reshape(G_A, CH_A).astype(jnp.float32) + b.reshape(G_A, CH_A).astype(jnp.float32)
    return y.astype(v.dtype)


def spatial_gate(u, vn, ws, bs, n):
    mask = jnp.tril(jnp.ones((n, n), dtype=bool))
    w = jnp.where(mask[None], ws[:, :n, :n], jnp.zeros((), ws.dtype))
    s = jnp.einsum('gts,bcsgd->bctgd', w, vn) + jnp.transpose(bs[:, :n])[None, None, :, :, None]
    return u * s


def t5_bucket(dist):
    n = jnp.maximum(dist, 0)
    nf = jnp.maximum(n, 1).astype(jnp.float32)
    large = MAX_EXACT + (jnp.log(nf / MAX_EXACT) / math.log(MAX_DISTANCE / MAX_EXACT)
                         * (NUM_BUCKETS - MAX_EXACT)).astype(jnp.int32)
    large = jnp.minimum(large, NUM_BUCKETS - 1)
    return jnp.where(n < MAX_EXACT, n, large)


def diff_attention(q, k, v, qpos, kpos, lambda_q1, lambda_k1, lambda_q2, lambda_k2, subln_g, rel_bias):
    dist = qpos[:, None] - kpos[None, :]
    bias = jnp.transpose(rel_bias[t5_bucket(dist)], (2, 0, 1)).astype(jnp.float32)
    mask = dist >= 0

    def attn_map(qi, ki):
        s = jnp.einsum('bqhd,bkhd->bhqk', qi, ki).astype(jnp.float32) * SCALE + bias
        s = jnp.where(mask, s, -1e30)
        return jax.nn.softmax(s, axis=-1)

    lam = (jnp.exp(jnp.sum(lambda_q1.astype(jnp.float32) * lambda_k1.astype(jnp.float32)))
           - jnp.exp(jnp.sum(lambda_q2.astype(jnp.float32) * lambda_k2.astype(jnp.float32)))
           + LAMBDA_INIT)
    a = attn_map(q[..., :DH], k[..., :DH]) - lam * attn_map(q[..., DH:], k[..., DH:])
    o = jnp.einsum('bhqk,bkhd->bqhd', a.astype(v.dtype), v)
    o = rmsnorm(o, subln_g) * (1.0 - LAMBDA_INIT)
    return o.reshape(o.shape[0], o.shape[1], D_B)


def channel_tail(x, mix, w_out, norm2_g, w_gate, w_up, w_down, final_g):
    h = x + jnp.einsum('bse,ed->bsd', mix, w_out)
    hn = rmsnorm(h, norm2_g)
    f = jax.nn.silu(jnp.einsum('bsd,df->bsf', hn, w_gate)) * jnp.einsum('bsd,df->bsf', hn, w_up)
    h = h + jnp.einsum('bsf,fd->bsd', f, w_down)
    return rmsnorm(h, final_g)


def setup_inputs(seed: int = 0) -> dict:
    key = jax.random.key(seed)
    ks = jax.random.split(key, 24)
    n_pages = PAST_LEN // PAGE_SIZE
    n_used = DEC_BATCH * n_pages
    n_pool = n_used + (n_used + 3) // 4
    f32 = jnp.float32
    page_table = jax.random.permutation(ks[4], n_pool)[:n_used].reshape(DEC_BATCH, n_pages).astype(jnp.int32)
    return {
        "x_prompt": jax.random.normal(ks[0], (BATCH, SEQ, D_MODEL), f32),
        "x_sample": jax.random.normal(ks[1], (DEC_BATCH, DEC_SEQ, D_MODEL), f32),
        "cache_k": jax.random.normal(ks[2], (n_pool, PAGE_SIZE, H_B, 2 * DH), f32),
        "cache_v": jax.random.normal(ks[3], (n_pool, PAGE_SIZE, H_B, DV), f32),
        "page_table": page_table,
        "norm1_g": 1.0 + 0.02 * jax.random.normal(ks[5], (D_MODEL,), f32),
        "w_in": jax.random.normal(ks[6], (D_MODEL, D_IN), f32) * D_MODEL ** -0.5,
        "gmlp_ln_g": 1.0 + 0.02 * jax.random.normal(ks[7], (D_A,), f32),
        "gmlp_ln_b": 0.02 * jax.random.normal(ks[8], (D_A,), f32),
        "gmlp_ws": jax.random.normal(ks[9], (G_A, CHUNK, CHUNK), f32) * CHUNK ** -0.5,
        "gmlp_bs": 1.0 + 0.02 * jax.random.normal(ks[10], (G_A, CHUNK), f32),
        "lambda_q1": 0.1 * jax.random.normal(ks[11], (DH,), f32),
        "lambda_k1": 0.1 * jax.random.normal(ks[12], (DH,), f32),
        "lambda_q2": 0.1 * jax.random.normal(ks[13], (DH,), f32),
        "lambda_k2": 0.1 * jax.random.normal(ks[14], (DH,), f32),
        "subln_g": 1.0 + 0.02 * jax.random.normal(ks[15], (DV,), f32),
        "rel_bias": 0.5 * jax.random.normal(ks[16], (NUM_BUCKETS, H_B), f32),
        "w_out": jax.random.normal(ks[17], (D_MIX, D_MODEL), f32) * D_MIX ** -0.5,
        "norm2_g": 1.0 + 0.02 * jax.random.normal(ks[18], (D_MODEL,), f32),
        "w_gate": jax.random.normal(ks[19], (D_MODEL, D_FF), f32) * D_MODEL ** -0.5,
        "w_up": jax.random.normal(ks[20], (D_MODEL, D_FF), f32) * D_MODEL ** -0.5,
        "w_down": jax.random.normal(ks[21], (D_FF, D_MODEL), f32) * D_FF ** -0.5,
        "final_g": 1.0 + 0.02 * jax.random.normal(ks[22], (D_MODEL,), f32),
    }


def reference(x_prompt, x_sample, cache_k, cache_v, page_table, norm1_g, w_in, gmlp_ln_g, gmlp_ln_b,
              gmlp_ws, gmlp_bs, lambda_q1, lambda_k1, lambda_q2, lambda_k2, subln_g, rel_bias,
              w_out, norm2_g, w_gate, w_up, w_down, final_g):
    attn_params = (lambda_q1, lambda_k1, lambda_q2, lambda_k2, subln_g, rel_bias)
    y_p, y_s = x_prompt, x_sample
    for _ in range(DEPTH):
        xn = rmsnorm(y_p, norm1_g)
        u, v, q, k_p, v_p = split_proj(xn, w_in)
        vn = group_layernorm(v, gmlp_ln_g, gmlp_ln_b)
        n_chunks = SEQ // CHUNK
        a_out = spatial_gate(u.reshape(BATCH, n_chunks, CHUNK, G_A, CH_A),
                             vn.reshape(BATCH, n_chunks, CHUNK, G_A, CH_A), gmlp_ws, gmlp_bs, CHUNK)
        a_out = a_out.reshape(BATCH, SEQ, D_A)
        n_qb = SEQ // Q_BLOCK
        q_blocks = jnp.transpose(q.reshape(BATCH, n_qb, Q_BLOCK, H_B, 2 * DH), (1, 0, 2, 3, 4))
        kpos_p = jnp.arange(SEQ, dtype=jnp.int32)

        def one_block(args):
            i, qi = args
            qpos = i * Q_BLOCK + jnp.arange(Q_BLOCK, dtype=jnp.int32)
            return diff_attention(qi, k_p, v_p, qpos, kpos_p, *attn_params)

        b_out = lax.map(one_block, (jnp.arange(n_qb, dtype=jnp.int32), q_blocks))
        b_out = jnp.transpose(b_out, (1, 0, 2, 3)).reshape(BATCH, SEQ, D_B)
        y_p = channel_tail(y_p, jnp.concatenate([a_out, b_out], axis=-1),
                           w_out, norm2_g, w_gate, w_up, w_down, final_g)

        xn_s = rmsnorm(y_s, norm1_g)
        u_s, v_s, q_s, k_s, vb_s = split_proj(xn_s, w_in)
        gv_s = group_layernorm(v_s, gmlp_ln_g, gmlp_ln_b)
        a_s = spatial_gate(u_s[:, None], gv_s[:, None], gmlp_ws, gmlp_bs, DEC_SEQ)
        a_s = a_s.reshape(DEC_BATCH, DEC_SEQ, D_A)
        past = page_table.shape[1] * PAGE_SIZE
        pk = cache_k[page_table].reshape(DEC_BATCH, past, H_B, 2 * DH).astype(k_s.dtype)
        pv = cache_v[page_table].reshape(DEC_BATCH, past, H_B, DV).astype(vb_s.dtype)
        k_all = jnp.concatenate([pk, k_s], axis=1)
        v_all = jnp.concatenate([pv, vb_s], axis=1)
        qpos_s = PAST_LEN + jnp.arange(DEC_SEQ, dtype=jnp.int32)
        kpos_s = jnp.arange(PAST_LEN + DEC_SEQ, dtype=jnp.int32)
        b_s = diff_attention(q_s, k_all, v_all, qpos_s, kpos_s, *attn_params)
        y_s = channel_tail(y_s, jnp.concatenate([a_s, b_s], axis=-1),
                           w_out, norm2_g, w_gate, w_up, w_down, final_g)
        gv_sample = gv_s.reshape(DEC_BATCH, DEC_SEQ, D_A)
    return (y_p, y_s, k_p, v_p, k_s, vb_s, gv_sample)
```

```python
import functools
import math

import numpy as np
import jax
import jax.numpy as jnp
from jax import lax
from jax.experimental import pallas as pl
from jax.experimental.pallas import tpu as pltpu

F32 = jnp.float32
BF16 = jnp.bfloat16

EPS = 1e-6
LANES = 128
G_A = 4
CH_A = 128
CHUNK = 128
H_B = 4
DV = 128
DH = 64
D_A = G_A * CH_A
D_QK = H_B * 2 * DH
D_B = H_B * DV
SCALE = DH ** -0.5
LAMBDA_INIT = 0.8 - 0.6 * math.exp(-0.3 * 0)
NUM_BUCKETS = 32
MAX_EXACT = NUM_BUCKETS // 2
MAX_DISTANCE = 128
PAGE = 128
NEG = -1e30
VMEM_LIMIT = 56 * 1024 * 1024


def _bucket_thresholds():
    n = np.arange(0, 2 * MAX_DISTANCE)
    nf = np.maximum(n, 1).astype(np.float64)
    large = MAX_EXACT + (np.log(nf / MAX_EXACT) / math.log(MAX_DISTANCE / MAX_EXACT)
                         * (NUM_BUCKETS - MAX_EXACT)).astype(np.int32)
    bucket = np.where(n < MAX_EXACT, n, np.minimum(large, NUM_BUCKETS - 1))
    assert np.all(np.diff(bucket) >= 0) and bucket[MAX_DISTANCE - 1] == NUM_BUCKETS - 1
    return [int(n[bucket >= b].min()) for b in range(NUM_BUCKETS)]


BUCKET_START = _bucket_thresholds()


def _rms(x, g):
    ms = jnp.mean(x * x, axis=-1, keepdims=True)
    return x * lax.rsqrt(ms + EPS) * g


def _dot(a, b):
    return jnp.dot(a, b, preferred_element_type=F32)


def _dot_nt(a, b):
    return lax.dot_general(a, b, (((1,), (1,)), ((), ())), preferred_element_type=F32)


def _bias_from_distance(n, value_of_bucket):
    val = jnp.zeros(n.shape, F32) + value_of_bucket(0)
    for b in range(1, NUM_BUCKETS):
        val = jnp.where(n >= BUCKET_START[b], value_of_bucket(b), val)
    return val


def _lambda_full(lam_ref):
    s1 = jnp.sum(lam_ref[0:1, :] * lam_ref[1:2, :], axis=-1, keepdims=True)
    s2 = jnp.sum(lam_ref[2:3, :] * lam_ref[3:4, :], axis=-1, keepdims=True)
    return jnp.exp(s1) - jnp.exp(s2) + LAMBDA_INIT


def _inproj_kernel(x_ref, g1_ref, win_ref, lng_ref, lnb_ref, ws_ref, bs_ref, *out_refs, cs, emit_vn):
    if emit_vn:
        a_ref, q_ref, kf_ref, vf_ref, kb_ref, vb_ref, vn_ref = out_refs
    else:
        a_ref, q_ref, kf_ref, vf_ref, kb_ref, vb_ref = out_refs
    tm = x_ref.shape[0]
    xn = _rms(x_ref[...], g1_ref[...]).astype(BF16)

    q = _dot(xn, win_ref[:, 2 * D_A:2 * D_A + D_QK])
    q_ref[...] = (q * SCALE).astype(BF16)
    k = _dot(xn, win_ref[:, 2 * D_A + D_QK:2 * D_A + 2 * D_QK])
    kf_ref[...] = k
    kb_ref[...] = k.astype(BF16)
    vb = _dot(xn, win_ref[:, 2 * D_A + 2 * D_QK:])
    vf_ref[...] = vb
    vb_ref[...] = vb.astype(BF16)

    u = _dot(xn, win_ref[:, 0:D_A])
    v = _dot(xn, win_ref[:, D_A:2 * D_A])
    row = lax.broadcasted_iota(jnp.int32, (cs, cs), 0)
    col = lax.broadcasted_iota(jnp.int32, (cs, cs), 1)
    for g in range(G_A):
        gs = slice(g * CH_A, (g + 1) * CH_A)
        vg = v[:, gs]
        mu = jnp.mean(vg, axis=-1, keepdims=True)
        d = vg - mu
        var = jnp.mean(d * d, axis=-1, keepdims=True)
        vn = d * lax.rsqrt(var + EPS) * lng_ref[:, gs] + lnb_ref[:, gs]
        if emit_vn:
            vn_ref[:, gs] = vn
        vn16 = vn.astype(BF16)
        w = jnp.where(row >= col, ws_ref[g], 0.0).astype(BF16)
        for c in range(tm // cs):
            rs = slice(c * cs, (c + 1) * cs)
            s = _dot(w, vn16[rs, :]) + bs_ref[:, gs]
            a_ref[rs, gs] = (u[rs, gs] * s).astype(BF16)


def _in_proj(x, g1, win16, lng, lnb, ws, bs_full, *, tm, cs, emit_vn):
    t, d = x.shape
    d_in = win16.shape[1]
    const = lambda i: (0, 0)
    tile = lambda w: pl.BlockSpec((tm, w), lambda i: (i, 0))
    out_shape = [jax.ShapeDtypeStruct((t, D_A), BF16), jax.ShapeDtypeStruct((t, D_QK), BF16),
                 jax.ShapeDtypeStruct((t, D_QK), F32), jax.ShapeDtypeStruct((t, D_B), F32),
                 jax.ShapeDtypeStruct((t, D_QK), BF16), jax.ShapeDtypeStruct((t, D_B), BF16)]
    out_specs = [tile(D_A), tile(D_QK), tile(D_QK), tile(D_B), tile(D_QK), tile(D_B)]
    if emit_vn:
        out_shape.append(jax.ShapeDtypeStruct((t, D_A), F32))
        out_specs.append(tile(D_A))
    return pl.pallas_call(
        functools.partial(_inproj_kernel, cs=cs, emit_vn=emit_vn),
        grid=(t // tm,),
        in_specs=[tile(d), pl.BlockSpec((1, d), const), pl.BlockSpec((d, d_in), const),
                  pl.BlockSpec((1, D_A), const), pl.BlockSpec((1, D_A), const),
                  pl.BlockSpec((G_A, cs, cs), lambda i: (0, 0, 0)), pl.BlockSpec((cs, D_A), const)],
        out_specs=out_specs,
        out_shape=out_shape,
        compiler_params=pltpu.CompilerParams(dimension_semantics=("arbitrary",),
                                             vmem_limit_bytes=VMEM_LIMIT),
        name="in_proj",
    )(x, g1, win16, lng, lnb, ws, bs_full)


def _attn_kernel(rb_ref, q_ref, k_ref, v_ref, lam_ref, sg_ref, o_ref,
                 bias_scr, qz_scr, m_scr, l_scr, acc_scr, *, t):
    b, h, qi = pl.program_id(0), pl.program_id(1), pl.program_id(2)

    @pl.when((b == 0) & (h == 0) & (qi == 0))
    def _build_bias_tiles():
        d0 = (lax.broadcasted_iota(jnp.int32, (t, t), 0) - lax.broadcasted_iota(jnp.int32, (t, t), 1))
        for hh in range(H_B):
            near = _bias_from_distance(jnp.maximum(d0, 0), lambda bk: rb_ref[bk, hh])
            bias_scr[hh, 0] = jnp.where(d0 >= 0, near, NEG)
            bias_scr[hh, 1] = _bias_from_distance(d0 + t, lambda bk: rb_ref[bk, hh])

    q = q_ref[0]
    lane = lax.broadcasted_iota(jnp.int32, (t, LANES), 1)
    qz_scr[0:t, :] = jnp.where(lane < DH, q, jnp.zeros_like(q))
    qz_scr[t:2 * t, :] = jnp.where(lane >= DH, q, jnp.zeros_like(q))

    def scores(kj, tile):
        k = k_ref[0, pl.ds(pl.multiple_of(kj * t, t), t), :]
        s = _dot_nt(qz_scr[...], k)
        if tile is not None:
            s = (s.reshape(2, t, t) + bias_scr[h, tile][None]).reshape(2 * t, t)
        return s

    def pv(p, kj):
        v = v_ref[0, pl.ds(pl.multiple_of(kj * t, t), t), :]
        return _dot(p.astype(BF16), v)

    def update(s, shift, kj):
        m_old = m_scr[...]
        m_new = jnp.maximum(m_old, jnp.max(s, axis=-1, keepdims=True) + shift)
        alpha = jnp.exp(m_old - m_new)
        p = jnp.exp(s - (m_new - shift))
        l_scr[...] = alpha * l_scr[...] + jnp.sum(p, axis=-1, keepdims=True)
        acc_scr[...] = alpha * acc_scr[...] + pv(p, kj)
        m_scr[...] = m_new

    s = scores(qi, 0)
    m0 = jnp.max(s, axis=-1, keepdims=True)
    p = jnp.exp(s - m0)
    m_scr[...] = m0
    l_scr[...] = jnp.sum(p, axis=-1, keepdims=True)
    acc_scr[...] = pv(p, qi)

    @pl.when(qi > 0)
    def _previous_block():
        update(scores(qi - 1, 1), 0.0, qi - 1)

    far_bias = rb_ref[NUM_BUCKETS - 1, h]

    def far_body(kj, carry):
        update(scores(kj, None), far_bias, kj)
        return carry

    lax.fori_loop(0, jnp.maximum(qi - 1, 0), far_body, 0)

    o = acc_scr[...] / l_scr[...]
    o = o[0:t, :] - _lambda_full(lam_ref) * o[t:2 * t, :]
    o_ref[0] = (_rms(o, sg_ref[...]) * (1.0 - LAMBDA_INIT)).astype(o_ref.dtype)


def _prompt_attention(rel_bias, q, k, v, lam, sg, *, t):
    b, s, _ = q.shape
    const = lambda bi, hi, qi: (0, 0)
    return pl.pallas_call(
        functools.partial(_attn_kernel, t=t),
        grid=(b, H_B, s // t),
        in_specs=[pl.BlockSpec(memory_space=pltpu.SMEM),
                  pl.BlockSpec((1, t, LANES), lambda bi, hi, qi: (bi, qi, hi)),
                  pl.BlockSpec((1, s, LANES), lambda bi, hi, qi: (bi, 0, hi)),
                  pl.BlockSpec((1, s, LANES), lambda bi, hi, qi: (bi, 0, hi)),
                  pl.BlockSpec((4, DH), const), pl.BlockSpec((1, DV), const)],
        out_specs=pl.BlockSpec((1, t, LANES), lambda bi, hi, qi: (bi, qi, hi)),
        out_shape=jax.ShapeDtypeStruct((b, s, D_B), BF16),
        scratch_shapes=[pltpu.VMEM((H_B, 2, t, t), F32), pltpu.VMEM((2 * t, LANES), BF16),
                        pltpu.VMEM((2 * t, 1), F32), pltpu.VMEM((2 * t, 1), F32),
                        pltpu.VMEM((2 * t, DV), F32)],
        compiler_params=pltpu.CompilerParams(dimension_semantics=("arbitrary",) * 3,
                                             vmem_limit_bytes=VMEM_LIMIT),
        name="prompt_attn",
    )(rel_bias, q, k, v, lam, sg)


def _paged_kernel(pt_ref, rb_ref, q_ref, kn_ref, vn_ref, lam_ref, sg_ref, *rest, pages, dec):
    k_refs, v_refs = rest[:pages], rest[pages:2 * pages]
    o_ref, qbd_scr, new_scr, tile_scr, far_scr, m_scr, l_scr, acc_scr = rest[2 * pages:]
    del pt_ref
    b, st = pl.program_id(0), pl.program_id(1)
    n_st = pl.num_programs(1)
    rows = 2 * H_B * dec

    @pl.when((b == 0) & (st == 0))
    def _build_bias_tiles():
        r = lax.broadcasted_iota(jnp.int32, (rows, PAGE), 0)
        j = lax.broadcasted_iota(jnp.int32, (rows, PAGE), 1)
        head, i = r // (2 * dec), r % dec

        def row_value(bk):
            val = jnp.zeros((rows, PAGE), F32) + rb_ref[bk, H_B - 1]
            for hh in range(H_B - 2, -1, -1):
                val = jnp.where(head == hh, rb_ref[bk, hh], val)
            return val

        tile_scr[0] = _bias_from_distance(PAGE + i - j, row_value)
        tile_scr[1] = jnp.where(j <= i, _bias_from_distance(jnp.maximum(i - j, 0), row_value), NEG)
        far_scr[...] = row_value(NUM_BUCKETS - 1)[:, 0:1]
        new_scr[...] = jnp.zeros_like(new_scr)

    @pl.when(st == 0)
    def _start_batch():
        q = jnp.concatenate([q_ref[0].astype(F32)] * (2 * H_B), axis=0)
        r = lax.broadcasted_iota(jnp.int32, q.shape, 0)
        lane = lax.broadcasted_iota(jnp.int32, q.shape, 1)
        qbd_scr[...] = jnp.where(lane // DH == r // dec, q, 0.0).astype(BF16)
        m_scr[...] = jnp.full_like(m_scr, -jnp.inf)
        l_scr[...] = jnp.zeros_like(l_scr)
        acc_scr[...] = jnp.zeros_like(acc_scr)

    def update(s, shift, v):
        m_old = m_scr[...]
        m_new = jnp.maximum(m_old, jnp.max(s, axis=-1, keepdims=True) + shift)
        alpha = jnp.exp(m_old - m_new)
        p = jnp.exp(s - (m_new - shift))
        l_scr[...] = alpha * l_scr[...] + jnp.sum(p, axis=-1, keepdims=True)
        acc_scr[...] = alpha * acc_scr[...] + _dot(p.astype(BF16), v)
        m_scr[...] = m_new

    def gather(refs, n):
        return jnp.concatenate([refs[p][0].astype(BF16) for p in range(n)], axis=0)

    qbd = qbd_scr[...]

    @pl.when(st < n_st - 1)
    def _far_pages():
        update(_dot_nt(qbd, gather(k_refs, pages)), far_scr[...], gather(v_refs, pages))

    @pl.when(st == n_st - 1)
    def _last_pages_and_new_rows():
        if pages > 1:
            update(_dot_nt(qbd, gather(k_refs, pages - 1)), far_scr[...], gather(v_refs, pages - 1))
        update(_dot_nt(qbd, k_refs[pages - 1][0].astype(BF16)) + tile_scr[0], 0.0,
               v_refs[pages - 1][0].astype(BF16))
        new_scr[0, 0:dec, :] = kn_ref[0]
        new_scr[1, 0:dec, :] = vn_ref[0]
        update(_dot_nt(qbd, new_scr[0].astype(BF16)) + tile_scr[1], 0.0, new_scr[1].astype(BF16))

        o = acc_scr[...] / l_scr[...]
        lam = _lambda_full(lam_ref)
        for hh in range(H_B):
            r0, ls = hh * 2 * dec, slice(hh * DV, (hh + 1) * DV)
            oh = o[r0:r0 + dec, ls] - lam * o[r0 + dec:r0 + 2 * dec, ls]
            o_ref[0, :, ls] = (_rms(oh, sg_ref[...]) * (1.0 - LAMBDA_INIT)).astype(o_ref.dtype)


def _sample_attention(page_table, rel_bias, q, k_new, v_new, cache_k, cache_v, lam, sg, *, pages):
    nb, dec, _ = q.shape
    n_pages = page_table.shape[1]
    n_st = n_pages // pages
    rows = 2 * H_B * dec
    const2 = lambda bi, si, pt: (0, 0)
    per_batch = pl.BlockSpec((1, dec, D_QK), lambda bi, si, pt: (bi, 0, 0))

    def page_spec(p):
        return pl.BlockSpec((1, PAGE, D_QK),
                            lambda bi, si, pt: (pt[bi * n_pages + si * pages + p], 0, 0))

    grid_spec = pltpu.PrefetchScalarGridSpec(
        num_scalar_prefetch=1,
        grid=(nb, n_st),
        in_specs=[pl.BlockSpec(memory_space=pltpu.SMEM), per_batch, per_batch, per_batch,
                  pl.BlockSpec((4, DH), const2), pl.BlockSpec((1, DV), const2)]
                 + [page_spec(p) for p in range(pages)] * 2,
        out_specs=per_batch,
        scratch_shapes=[pltpu.VMEM((rows, D_QK), BF16), pltpu.VMEM((2, PAGE, D_QK), F32),
                        pltpu.VMEM((2, rows, PAGE), F32), pltpu.VMEM((rows, 1), F32),
                        pltpu.VMEM((rows, 1), F32), pltpu.VMEM((rows, 1), F32),
                        pltpu.VMEM((rows, D_B), F32)],
    )
    return pl.pallas_call(
        functools.partial(_paged_kernel, pages=pages, dec=dec),
        grid_spec=grid_spec,
        out_shape=jax.ShapeDtypeStruct((nb, dec, D_B), BF16),
        compiler_params=pltpu.CompilerParams(dimension_semantics=("arbitrary", "arbitrary"),
                                             vmem_limit_bytes=VMEM_LIMIT),
        name="paged_attn",
    )(page_table.reshape(-1), rel_bias, q, k_new, v_new, lam, sg,
      *([cache_k] * pages), *([cache_v] * pages))


def _tail_kernel(x_ref, a_ref, b_ref, wo_ref, g2_ref, wg_ref, wu_ref, wd_ref, gf_ref, y_ref):
    h = x_ref[...] + _dot(a_ref[...], wo_ref[0:D_A, :]) + _dot(b_ref[...], wo_ref[D_A:, :])
    hn = _rms(h, g2_ref[...]).astype(BF16)
    gate = _dot(hn, wg_ref[...])
    f = (gate * (1.0 / (1.0 + jnp.exp(-gate))) * _dot(hn, wu_ref[...])).astype(BF16)
    y_ref[...] = _rms(h + _dot(f, wd_ref[...]), gf_ref[...])


def _tail(x, a, bmix, wo16, g2, wg16, wu16, wd16, gf, *, tm):
    t, d = x.shape
    d_ff = wg16.shape[1]
    const = lambda i: (0, 0)
    resident = lambda shape: pl.BlockSpec(shape, const, pipeline_mode=pl.Buffered(1))
    return pl.pallas_call(
        _tail_kernel,
        grid=(t // tm,),
        in_specs=[pl.BlockSpec((tm, d), lambda i: (i, 0)), pl.BlockSpec((tm, D_A), lambda i: (i, 0)),
                  pl.BlockSpec((tm, D_B), lambda i: (i, 0)), resident((D_A + D_B, d)),
                  pl.BlockSpec((1, d), const), resident((d, d_ff)), resident((d, d_ff)),
                  resident((d_ff, d)), pl.BlockSpec((1, d), const)],
        out_specs=pl.BlockSpec((tm, d), lambda i: (i, 0)),
        out_shape=jax.ShapeDtypeStruct((t, d), F32),
        compiler_params=pltpu.CompilerParams(dimension_semantics=("arbitrary",),
                                             vmem_limit_bytes=VMEM_LIMIT),
        name="tail",
    )(x, a, bmix, wo16, g2, wg16, wu16, wd16, gf)


def kernel(x_prompt, x_sample, cache_k, cache_v, page_table, norm1_g, w_in, gmlp_ln_g, gmlp_ln_b, gmlp_ws, gmlp_bs, lambda_q1, lambda_k1, lambda_q2, lambda_k2, subln_g, rel_bias, w_out, norm2_g, w_gate, w_up, w_down, final_g):
    nb, seq, d = x_prompt.shape
    db, dec, _ = x_sample.shape
    n_pool = cache_k.shape[0]
    row = lambda p: p.reshape(1, -1)

    win16, wo16 = w_in.astype(BF16), w_out.astype(BF16)
    wg16, wu16, wd16 = w_gate.astype(BF16), w_up.astype(BF16), w_down.astype(BF16)
    lam = jnp.stack([lambda_q1, lambda_k1, lambda_q2, lambda_k2])
    proj_params = (row(norm1_g), win16, row(gmlp_ln_g), row(gmlp_ln_b))
    tail_params = (wo16, row(norm2_g), wg16, wu16, wd16, row(final_g))

    xp = x_prompt.reshape(nb * seq, d)
    bs_p = jnp.repeat(gmlp_bs.T, CH_A, axis=1)
    a_p, q_p, kf_p, vf_p, kb_p, vb_p = _in_proj(xp, *proj_params, gmlp_ws, bs_p,
                                               tm=512, cs=CHUNK, emit_vn=False)
    shape3 = lambda z: z.reshape(nb, seq, -1)
    b_p = _prompt_attention(rel_bias, shape3(q_p), shape3(kb_p), shape3(vb_p), lam, row(subln_g), t=256)
    y_p = _tail(xp, a_p, b_p.reshape(nb * seq, D_B), *tail_params, tm=512)

    xs = x_sample.reshape(db * dec, d)
    ws_s = jnp.einsum('ab,gts->gatbs', jnp.eye(db, dtype=F32),
                      gmlp_ws[:, :dec, :dec]).reshape(G_A, db * dec, db * dec)
    bs_s = jnp.tile(jnp.repeat(gmlp_bs[:, :dec].T, CH_A, axis=1), (db, 1))
    a_s, q_s, kf_s, vf_s, _, _, gv_s = _in_proj(xs, *proj_params, ws_s, bs_s,
                                               tm=db * dec, cs=db * dec, emit_vn=True)
    b_s = _sample_attention(page_table, rel_bias, q_s.reshape(db, dec, D_QK),
                            kf_s.reshape(db, dec, D_QK), vf_s.reshape(db, dec, D_B),
                            cache_k.reshape(n_pool, PAGE, D_QK), cache_v.reshape(n_pool, PAGE, D_B),
                            lam, row(subln_g), pages=8)
    y_s = _tail(xs, a_s, b_s.reshape(db * dec, D_B), *tail_params, tm=db * dec)

    return (y_p.reshape(nb, seq, d), y_s.reshape(db, dec, d),
            kf_p.reshape(nb, seq, H_B, 2 * DH), vf_p.reshape(nb, seq, H_B, DV),
            kf_s.reshape(db, dec, H_B, 2 * DH), vf_s.reshape(db, dec, H_B, DV),
            gv_s.reshape(db, dec, D_A))
```

```python
import functools
import math

import numpy as np
import jax
import jax.numpy as jnp
from jax import lax
from jax.experimental import pallas as pl
from jax.experimental.pallas import tpu as pltpu

F32 = jnp.float32
BF16 = jnp.bfloat16

EPS = 1e-6
LANES = 128
G_A = 4
CH_A = 128
CHUNK = 128
H_B = 4
DV = 128
DH = 64
D_A = G_A * CH_A
D_QK = H_B * 2 * DH
D_B = H_B * DV
SCALE = DH ** -0.5
LAMBDA_INIT = 0.8 - 0.6 * math.exp(-0.3 * 0)
NUM_BUCKETS = 32
MAX_EXACT = NUM_BUCKETS // 2
MAX_DISTANCE = 128
PAGE = 128
NEG = -1e30
VMEM_LIMIT = 56 * 1024 * 1024


def _bucket_thresholds():
    n = np.arange(0, 2 * MAX_DISTANCE)
    nf = np.maximum(n, 1).astype(np.float64)
    large = MAX_EXACT + (np.log(nf / MAX_EXACT) / math.log(MAX_DISTANCE / MAX_EXACT)
                         * (NUM_BUCKETS - MAX_EXACT)).astype(np.int32)
    bucket = np.where(n < MAX_EXACT, n, np.minimum(large, NUM_BUCKETS - 1))
    assert np.all(np.diff(bucket) >= 0) and bucket[MAX_DISTANCE - 1] == NUM_BUCKETS - 1
    return [int(n[bucket >= b].min()) for b in range(NUM_BUCKETS)]


BUCKET_START = _bucket_thresholds()


def _rms(x, g):
    ms = jnp.mean(x * x, axis=-1, keepdims=True)
    return x * lax.rsqrt(ms + EPS) * g


def _dot(a, b):
    return jnp.dot(a, b, preferred_element_type=F32)


def _dot_nt(a, b):
    return lax.dot_general(a, b, (((1,), (1,)), ((), ())), preferred_element_type=F32)


def _bias_from_distance(n, value_of_bucket):
    val = jnp.zeros(n.shape, F32) + value_of_bucket(0)
    for b in range(1, NUM_BUCKETS):
        val = jnp.where(n >= BUCKET_START[b], value_of_bucket(b), val)
    return val


def _lambda_full(lam_ref):
    s1 = jnp.sum(lam_ref[0:1, :] * lam_ref[1:2, :], axis=-1, keepdims=True)
    s2 = jnp.sum(lam_ref[2:3, :] * lam_ref[3:4, :], axis=-1, keepdims=True)
    return jnp.exp(s1) - jnp.exp(s2) + LAMBDA_INIT


def _inproj_kernel(x_ref, g1_ref, win_ref, lng_ref, lnb_ref, ws_ref, bs_ref, *out_refs, cs, emit_vn):
    if emit_vn:
        a_ref, q_ref, kf_ref, vf_ref, kb_ref, vb_ref, vn_ref = out_refs
    else:
        a_ref, q_ref, kf_ref, vf_ref, kb_ref, vb_ref = out_refs
    tm = x_ref.shape[0]
    xn = _rms(x_ref[...], g1_ref[...]).astype(BF16)

    q = _dot(xn, win_ref[:, 2 * D_A:2 * D_A + D_QK])
    q_ref[...] = (q * SCALE).astype(BF16)
    k = _dot(xn, win_ref[:, 2 * D_A + D_QK:2 * D_A + 2 * D_QK])
    kf_ref[...] = k
    kb_ref[...] = k.astype(BF16)
    vb = _dot(xn, win_ref[:, 2 * D_A + 2 * D_QK:])
    vf_ref[...] = vb
    vb_ref[...] = vb.astype(BF16)

    u = _dot(xn, win_ref[:, 0:D_A])
    v = _dot(xn, win_ref[:, D_A:2 * D_A])
    row = lax.broadcasted_iota(jnp.int32, (cs, cs), 0)
    col = lax.broadcasted_iota(jnp.int32, (cs, cs), 1)
    for g in range(G_A):
        gs = slice(g * CH_A, (g + 1) * CH_A)
        vg = v[:, gs]
        mu = jnp.mean(vg, axis=-1, keepdims=True)
        d = vg - mu
        var = jnp.mean(d * d, axis=-1, keepdims=True)
        vn = d * lax.rsqrt(var + EPS) * lng_ref[:, gs] + lnb_ref[:, gs]
        if emit_vn:
            vn_ref[:, gs] = vn
        vn16 = vn.astype(BF16)
        w = jnp.where(row >= col, ws_ref[g], 0.0).astype(BF16)
        for c in range(tm // cs):
            rs = slice(c * cs, (c + 1) * cs)
            s = _dot(w, vn16[rs, :]) + bs_ref[:, gs]
            a_ref[rs, gs] = (u[rs, gs] * s).astype(BF16)


def _in_proj(x, g1, win16, lng, lnb, ws, bs_full, *, tm, cs, emit_vn):
    t, d = x.shape
    d_in = win16.shape[1]
    const = lambda i: (0, 0)
    tile = lambda w: pl.BlockSpec((tm, w), lambda i: (i, 0))
    out_shape = [jax.ShapeDtypeStruct((t, D_A), BF16), jax.ShapeDtypeStruct((t, D_QK), BF16),
                 jax.ShapeDtypeStruct((t, D_QK), F32), jax.ShapeDtypeStruct((t, D_B), F32),
                 jax.ShapeDtypeStruct((t, D_QK), BF16), jax.ShapeDtypeStruct((t, D_B), BF16)]
    out_specs = [tile(D_A), tile(D_QK), tile(D_QK), tile(D_B), tile(D_QK), tile(D_B)]
    if emit_vn:
        out_shape.append(jax.ShapeDtypeStruct((t, D_A), F32))
        out_specs.append(tile(D_A))
    return pl.pallas_call(
        functools.partial(_inproj_kernel, cs=cs, emit_vn=emit_vn),
        grid=(t // tm,),
        in_specs=[tile(d), pl.BlockSpec((1, d), const), pl.BlockSpec((d, d_in), const),
                  pl.BlockSpec((1, D_A), const), pl.BlockSpec((1, D_A), const),
                  pl.BlockSpec((G_A, cs, cs), lambda i: (0, 0, 0)), pl.BlockSpec((cs, D_A), const)],
        out_specs=out_specs,
        out_shape=out_shape,
        compiler_params=pltpu.CompilerParams(dimension_semantics=("arbitrary",),
                                             vmem_limit_bytes=VMEM_LIMIT),
        name="in_proj",
    )(x, g1, win16, lng, lnb, ws, bs_full)


def _attn_kernel(rb_ref, q_ref, k_ref, v_ref, lam_ref, sg_ref, o_ref,
                 bias_scr, qz_scr, m_scr, l_scr, acc_scr, *, t):
    b, h, qi = pl.program_id(0), pl.program_id(1), pl.program_id(2)

    @pl.when((b == 0) & (h == 0) & (qi == 0))
    def _build_bias_tiles():
        d0 = (lax.broadcasted_iota(jnp.int32, (t, t), 0) - lax.broadcasted_iota(jnp.int32, (t, t), 1))
        for hh in range(H_B):
            near = _bias_from_distance(jnp.maximum(d0, 0), lambda bk: rb_ref[bk, hh])
            bias_scr[hh, 0] = jnp.where(d0 >= 0, near, NEG)
            bias_scr[hh, 1] = _bias_from_distance(d0 + t, lambda bk: rb_ref[bk, hh])

    q = q_ref[0]
    lane = lax.broadcasted_iota(jnp.int32, (t, LANES), 1)
    qz_scr[0:t, :] = jnp.where(lane < DH, q, jnp.zeros_like(q))
    qz_scr[t:2 * t, :] = jnp.where(lane >= DH, q, jnp.zeros_like(q))

    def scores(kj, tile):
        k = k_ref[0, pl.ds(pl.multiple_of(kj * t, t), t), :]
        s = _dot_nt(qz_scr[...], k)
        if tile is not None:
            s = (s.reshape(2, t, t) + bias_scr[h, tile][None]).reshape(2 * t, t)
        return s

    def pv(p, kj):
        v = v_ref[0, pl.ds(pl.multiple_of(kj * t, t), t), :]
        return _dot(p.astype(BF16), v)

    def update(s, shift, kj):
        m_old = m_scr[...]
        m_new = jnp.maximum(m_old, jnp.max(s, axis=-1, keepdims=True) + shift)
        alpha = jnp.exp(m_old - m_new)
        p = jnp.exp(s - (m_new - shift))
        l_scr[...] = alpha * l_scr[...] + jnp.sum(p, axis=-1, keepdims=True)
        acc_scr[...] = alpha * acc_scr[...] + pv(p, kj)
        m_scr[...] = m_new

    s = scores(qi, 0)
    m0 = jnp.max(s, axis=-1, keepdims=True)
    p = jnp.exp(s - m0)
    m_scr[...] = m0
    l_scr[...] = jnp.sum(p, axis=-1, keepdims=True)
    acc_scr[...] = pv(p, qi)

    @pl.when(qi > 0)
    def _previous_block():
        update(scores(qi - 1, 1), 0.0, qi - 1)

    far_bias = rb_ref[NUM_BUCKETS - 1, h]

    def far_body(kj, carry):
        update(scores(kj, None), far_bias, kj)
        return carry

    lax.fori_loop(0, jnp.maximum(qi - 1, 0), far_body, 0)

    o = acc_scr[...] / l_scr[...]
    o = o[0:t, :] - _lambda_full(lam_ref) * o[t:2 * t, :]
    o_ref[0] = (_rms(o, sg_ref[...]) * (1.0 - LAMBDA_INIT)).astype(o_ref.dtype)


def _prompt_attention(rel_bias, q, k, v, lam, sg, *, t):
    b, s, _ = q.shape
    const = lambda bi, hi, qi: (0, 0)
    return pl.pallas_call(
        functools.partial(_attn_kernel, t=t),
        grid=(b, H_B, s // t),
        in_specs=[pl.BlockSpec(memory_space=pltpu.SMEM),
                  pl.BlockSpec((1, t, LANES), lambda bi, hi, qi: (bi, qi, hi)),
                  pl.BlockSpec((1, s, LANES), lambda bi, hi, qi: (bi, 0, hi)),
                  pl.BlockSpec((1, s, LANES), lambda bi, hi, qi: (bi, 0, hi)),
                  pl.BlockSpec((4, DH), const), pl.BlockSpec((1, DV), const)],
        out_specs=pl.BlockSpec((1, t, LANES), lambda bi, hi, qi: (bi, qi, hi)),
        out_shape=jax.ShapeDtypeStruct((b, s, D_B), BF16),
        scratch_shapes=[pltpu.VMEM((H_B, 2, t, t), F32), pltpu.VMEM((2 * t, LANES), BF16),
                        pltpu.VMEM((2 * t, 1), F32), pltpu.VMEM((2 * t, 1), F32),
                        pltpu.VMEM((2 * t, DV), F32)],
        compiler_params=pltpu.CompilerParams(dimension_semantics=("arbitrary",) * 3,
                                             vmem_limit_bytes=VMEM_LIMIT),
        name="prompt_attn",
    )(rel_bias, q, k, v, lam, sg)


def _paged_kernel(pt_ref, rb_ref, q_ref, kn_ref, vn_ref, lam_ref, sg_ref, *rest, pages, dec):
    k_refs, v_refs = rest[:pages], rest[pages:2 * pages]
    o_ref, qh_scr, new_scr, tile_scr, far_scr, m_scr, l_scr, acc_scr = rest[2 * pages:]
    del pt_ref
    b, st = pl.program_id(0), pl.program_id(1)
    n_st = pl.num_programs(1)
    hrows = 2 * dec
    rows = H_B * hrows

    @pl.when((b == 0) & (st == 0))
    def _build_bias_tiles():
        r = lax.broadcasted_iota(jnp.int32, (rows, PAGE), 0)
        j = lax.broadcasted_iota(jnp.int32, (rows, PAGE), 1)
        head, i = r // hrows, r % dec

        def row_value(bk):
            val = jnp.zeros((rows, PAGE), F32) + rb_ref[bk, H_B - 1]
            for hh in range(H_B - 2, -1, -1):
                val = jnp.where(head == hh, rb_ref[bk, hh], val)
            return val

        tile_scr[0] = _bias_from_distance(PAGE + i - j, row_value)
        tile_scr[1] = jnp.where(j <= i, _bias_from_distance(jnp.maximum(i - j, 0), row_value), NEG)
        far_scr[...] = row_value(NUM_BUCKETS - 1)[:, 0:1]
        new_scr[...] = jnp.zeros_like(new_scr)

    @pl.when(st == 0)
    def _start_batch():
        r = lax.broadcasted_iota(jnp.int32, (hrows, LANES), 0)
        lane = lax.broadcasted_iota(jnp.int32, (hrows, LANES), 1)
        for hh in range(H_B):
            qh = q_ref[0, :, hh * LANES:(hh + 1) * LANES].astype(F32)
            qh = jnp.concatenate([qh, qh], axis=0)
            qh_scr[hh] = jnp.where(lane // DH == r // dec, qh, 0.0).astype(BF16)
        m_scr[...] = jnp.full_like(m_scr, -jnp.inf)
        l_scr[...] = jnp.zeros_like(l_scr)
        acc_scr[...] = jnp.zeros_like(acc_scr)

    def scores(head_keys):
        return jnp.concatenate([_dot_nt(qh_scr[hh], head_keys(hh)) for hh in range(H_B)], axis=0)

    def update(s, shift, head_values):
        m_old = m_scr[...]
        m_new = jnp.maximum(m_old, jnp.max(s, axis=-1, keepdims=True) + shift)
        alpha = jnp.exp(m_old - m_new)
        p = jnp.exp(s - (m_new - shift))
        l_scr[...] = alpha * l_scr[...] + jnp.sum(p, axis=-1, keepdims=True)
        p = p.astype(BF16)
        pv = jnp.concatenate([_dot(p[hh * hrows:(hh + 1) * hrows], head_values(hh))
                              for hh in range(H_B)], axis=0)
        acc_scr[...] = alpha * acc_scr[...] + pv
        m_scr[...] = m_new

    def cached(refs, first, last):
        return lambda hh: jnp.concatenate(
            [refs[p][0, pl.ds(hh, PAGE, stride=H_B), :] for p in range(first, last)], axis=0).astype(BF16)

    def fresh(which):
        return lambda hh: new_scr[which, :, hh * LANES:(hh + 1) * LANES].astype(BF16)

    @pl.when(st < n_st - 1)
    def _far_pages():
        update(scores(cached(k_refs, 0, pages)), far_scr[...], cached(v_refs, 0, pages))

    @pl.when(st == n_st - 1)
    def _last_pages_and_new_rows():
        if pages > 1:
            update(scores(cached(k_refs, 0, pages - 1)), far_scr[...], cached(v_refs, 0, pages - 1))
        update(scores(cached(k_refs, pages - 1, pages)) + tile_scr[0], 0.0,
               cached(v_refs, pages - 1, pages))
        new_scr[0, 0:dec, :] = kn_ref[0]
        new_scr[1, 0:dec, :] = vn_ref[0]
        update(scores(fresh(0)) + tile_scr[1], 0.0, fresh(1))

        o = acc_scr[...] / l_scr[...]
        lam = _lambda_full(lam_ref)
        for hh in range(H_B):
            r0 = hh * hrows
            oh = o[r0:r0 + dec, :] - lam * o[r0 + dec:r0 + hrows, :]
            o_ref[0, :, hh * DV:(hh + 1) * DV] = (_rms(oh, sg_ref[...])
                                                  * (1.0 - LAMBDA_INIT)).astype(o_ref.dtype)


def _sample_attention(page_table, rel_bias, q, k_new, v_new, cache_k, cache_v, lam, sg, *, pages):
    nb, dec, _ = q.shape
    n_pages = page_table.shape[1]
    n_st = n_pages // pages
    rows = 2 * H_B * dec
    const2 = lambda bi, si, pt: (0, 0)
    per_batch = pl.BlockSpec((1, dec, D_QK), lambda bi, si, pt: (bi, 0, 0))

    def page_spec(p):
        return pl.BlockSpec((1, PAGE * H_B, LANES),
                            lambda bi, si, pt: (pt[bi * n_pages + si * pages + p], 0, 0))

    grid_spec = pltpu.PrefetchScalarGridSpec(
        num_scalar_prefetch=1,
        grid=(nb, n_st),
        in_specs=[pl.BlockSpec(memory_space=pltpu.SMEM), per_batch, per_batch, per_batch,
                  pl.BlockSpec((4, DH), const2), pl.BlockSpec((1, DV), const2)]
                 + [page_spec(p) for p in range(pages)] * 2,
        out_specs=per_batch,
        scratch_shapes=[pltpu.VMEM((H_B, 2 * dec, LANES), BF16), pltpu.VMEM((2, PAGE, D_QK), F32),
                        pltpu.VMEM((2, rows, PAGE), F32), pltpu.VMEM((rows, 1), F32),
                        pltpu.VMEM((rows, 1), F32), pltpu.VMEM((rows, 1), F32),
                        pltpu.VMEM((rows, DV), F32)],
    )
    return pl.pallas_call(
        functools.partial(_paged_kernel, pages=pages, dec=dec),
        grid_spec=grid_spec,
        out_shape=jax.ShapeDtypeStruct((nb, dec, D_B), BF16),
        compiler_params=pltpu.CompilerParams(dimension_semantics=("arbitrary", "arbitrary"),
                                             vmem_limit_bytes=VMEM_LIMIT),
        name="paged_attn",
    )(page_table.reshape(-1), rel_bias, q, k_new, v_new, lam, sg,
      *([cache_k] * pages), *([cache_v] * pages))


def _tail_kernel(x_ref, a_ref, b_ref, wo_ref, g2_ref, wg_ref, wu_ref, wd_ref, gf_ref, y_ref):
    h = x_ref[...] + _dot(a_ref[...], wo_ref[0:D_A, :]) + _dot(b_ref[...], wo_ref[D_A:, :])
    hn = _rms(h, g2_ref[...]).astype(BF16)
    gate = _dot(hn, wg_ref[...])
    f = (gate * (1.0 / (1.0 + jnp.exp(-gate))) * _dot(hn, wu_ref[...])).astype(BF16)
    y_ref[...] = _rms(h + _dot(f, wd_ref[...]), gf_ref[...])


def _tail(x, a, bmix, wo16, g2, wg16, wu16, wd16, gf, *, tm):
    t, d = x.shape
    d_ff = wg16.shape[1]
    const = lambda i: (0, 0)
    resident = lambda shape: pl.BlockSpec(shape, const, pipeline_mode=pl.Buffered(1))
    return pl.pallas_call(
        _tail_kernel,
        grid=(t // tm,),
        in_specs=[pl.BlockSpec((tm, d), lambda i: (i, 0)), pl.BlockSpec((tm, D_A), lambda i: (i, 0)),
                  pl.BlockSpec((tm, D_B), lambda i: (i, 0)), resident((D_A + D_B, d)),
                  pl.BlockSpec((1, d), const), resident((d, d_ff)), resident((d, d_ff)),
                  resident((d_ff, d)), pl.BlockSpec((1, d), const)],
        out_specs=pl.BlockSpec((tm, d), lambda i: (i, 0)),
        out_shape=jax.ShapeDtypeStruct((t, d), F32),
        compiler_params=pltpu.CompilerParams(dimension_semantics=("arbitrary",),
                                             vmem_limit_bytes=VMEM_LIMIT),
        name="tail",
    )(x, a, bmix, wo16, g2, wg16, wu16, wd16, gf)


def kernel(x_prompt, x_sample, cache_k, cache_v, page_table, norm1_g, w_in, gmlp_ln_g, gmlp_ln_b, gmlp_ws, gmlp_bs, lambda_q1, lambda_k1, lambda_q2, lambda_k2, subln_g, rel_bias, w_out, norm2_g, w_gate, w_up, w_down, final_g):
    nb, seq, d = x_prompt.shape
    db, dec, _ = x_sample.shape
    row = lambda p: p.reshape(1, -1)

    win16, wo16 = w_in.astype(BF16), w_out.astype(BF16)
    wg16, wu16, wd16 = w_gate.astype(BF16), w_up.astype(BF16), w_down.astype(BF16)
    lam = jnp.stack([lambda_q1, lambda_k1, lambda_q2, lambda_k2])
    proj_params = (row(norm1_g), win16, row(gmlp_ln_g), row(gmlp_ln_b))
    tail_params = (wo16, row(norm2_g), wg16, wu16, wd16, row(final_g))

    xp = x_prompt.reshape(nb * seq, d)
    bs_p = jnp.repeat(gmlp_bs.T, CH_A, axis=1)
    a_p, q_p, kf_p, vf_p, kb_p, vb_p = _in_proj(xp, *proj_params, gmlp_ws, bs_p,
                                               tm=512, cs=CHUNK, emit_vn=False)
    shape3 = lambda z: z.reshape(nb, seq, -1)
    b_p = _prompt_attention(rel_bias, shape3(q_p), shape3(kb_p), shape3(vb_p), lam, row(subln_g), t=256)
    y_p = _tail(xp, a_p, b_p.reshape(nb * seq, D_B), *tail_params, tm=512)

    xs = x_sample.reshape(db * dec, d)
    ws_s = jnp.einsum('ab,gts->gatbs', jnp.eye(db, dtype=F32),
                      gmlp_ws[:, :dec, :dec]).reshape(G_A, db * dec, db * dec)
    bs_s = jnp.tile(jnp.repeat(gmlp_bs[:, :dec].T, CH_A, axis=1), (db, 1))
    a_s, q_s, kf_s, vf_s, _, _, gv_s = _in_proj(xs, *proj_params, ws_s, bs_s,
                                               tm=db * dec, cs=db * dec, emit_vn=True)
    b_s = _sample_attention(page_table, rel_bias, q_s.reshape(db, dec, D_QK),
                            kf_s.reshape(db, dec, D_QK), vf_s.reshape(db, dec, D_B),
                            cache_k.reshape(-1, PAGE * H_B, DV), cache_v.reshape(-1, PAGE * H_B, DV),
                            lam, row(subln_g), pages=8)
    y_s = _tail(xs, a_s, b_s.reshape(db * dec, D_B), *tail_params, tm=db * dec)

    return (y_p.reshape(nb, seq, d), y_s.reshape(db, dec, d),
            kf_p.reshape(nb, seq, H_B, 2 * DH), vf_p.reshape(nb, seq, H_B, DV),
            kf_s.reshape(db, dec, H_B, 2 * DH), vf_s.reshape(db, dec, H_B, DV),
            gv_s.reshape(db, dec, D_A))
```

```python
import functools
import math

import numpy as np
import jax
import jax.numpy as jnp
from jax import lax
from jax.experimental import pallas as pl
from jax.experimental.pallas import tpu as pltpu

F32 = jnp.float32
BF16 = jnp.bfloat16

EPS = 1e-6
LANES = 128
G_A = 4
CH_A = 128
CHUNK = 128
H_B = 4
DV = 128
DH = 64
D_A = G_A * CH_A
D_QK = H_B * 2 * DH
D_B = H_B * DV
SCALE = DH ** -0.5
LAMBDA_INIT = 0.8 - 0.6 * math.exp(-0.3 * 0)
NUM_BUCKETS = 32
MAX_EXACT = NUM_BUCKETS // 2
MAX_DISTANCE = 128
PAGE = 128
NEG = -1e30
VMEM_LIMIT = 56 * 1024 * 1024


def _bucket_thresholds():
    n = np.arange(0, 2 * MAX_DISTANCE)
    nf = np.maximum(n, 1).astype(np.float64)
    large = MAX_EXACT + (np.log(nf / MAX_EXACT) / math.log(MAX_DISTANCE / MAX_EXACT)
                         * (NUM_BUCKETS - MAX_EXACT)).astype(np.int32)
    bucket = np.where(n < MAX_EXACT, n, np.minimum(large, NUM_BUCKETS - 1))
    assert np.all(np.diff(bucket) >= 0) and bucket[MAX_DISTANCE - 1] == NUM_BUCKETS - 1
    return [int(n[bucket >= b].min()) for b in range(NUM_BUCKETS)]


BUCKET_START = _bucket_thresholds()


def _rms(x, g):
    ms = jnp.mean(x * x, axis=-1, keepdims=True)
    return x * lax.rsqrt(ms + EPS) * g


def _dot(a, b):
    return jnp.dot(a, b, preferred_element_type=F32)


def _dot_nt(a, b):
    return lax.dot_general(a, b, (((1,), (1,)), ((), ())), preferred_element_type=F32)


def _bias_from_distance(n, value_of_bucket):
    val = jnp.zeros(n.shape, F32) + value_of_bucket(0)
    for b in range(1, NUM_BUCKETS):
        val = jnp.where(n >= BUCKET_START[b], value_of_bucket(b), val)
    return val


def _lambda_full(lam_ref):
    s1 = jnp.sum(lam_ref[0:1, :] * lam_ref[1:2, :], axis=-1, keepdims=True)
    s2 = jnp.sum(lam_ref[2:3, :] * lam_ref[3:4, :], axis=-1, keepdims=True)
    return jnp.exp(s1) - jnp.exp(s2) + LAMBDA_INIT


def _inproj_kernel(x_ref, g1_ref, win_ref, lng_ref, lnb_ref, ws_ref, bs_ref, *out_refs, cs, prompt):
    if prompt:
        a_ref, q_ref, kf_ref, vf_ref, kb_ref, vt_ref = out_refs
    else:
        a_ref, q_ref, kf_ref, vf_ref, vn_ref = out_refs
    tm = x_ref.shape[0]
    xn = _rms(x_ref[...], g1_ref[...]).astype(BF16)

    q = _dot(xn, win_ref[:, 2 * D_A:2 * D_A + D_QK])
    q_ref[...] = (q * SCALE).astype(BF16)
    k = _dot(xn, win_ref[:, 2 * D_A + D_QK:2 * D_A + 2 * D_QK])
    kf_ref[...] = k
    vb = _dot(xn, win_ref[:, 2 * D_A + 2 * D_QK:])
    vf_ref[...] = vb
    if prompt:
        kb_ref[...] = k.astype(BF16)
        vt_ref[0] = vb.T.astype(BF16)

    u = _dot(xn, win_ref[:, 0:D_A])
    v = _dot(xn, win_ref[:, D_A:2 * D_A])
    row = lax.broadcasted_iota(jnp.int32, (cs, cs), 0)
    col = lax.broadcasted_iota(jnp.int32, (cs, cs), 1)
    for g in range(G_A):
        gs = slice(g * CH_A, (g + 1) * CH_A)
        vg = v[:, gs]
        mu = jnp.mean(vg, axis=-1, keepdims=True)
        d = vg - mu
        var = jnp.mean(d * d, axis=-1, keepdims=True)
        vn = d * lax.rsqrt(var + EPS) * lng_ref[:, gs] + lnb_ref[:, gs]
        if not prompt:
            vn_ref[:, gs] = vn
        vn16 = vn.astype(BF16)
        w = jnp.where(row >= col, ws_ref[g], 0.0).astype(BF16)
        for c in range(tm // cs):
            rs = slice(c * cs, (c + 1) * cs)
            s = _dot(w, vn16[rs, :]) + bs_ref[:, gs]
            a_ref[rs, gs] = (u[rs, gs] * s).astype(BF16)


def _in_proj(x, g1, win16, lng, lnb, ws, bs_full, *, tm, cs, seq=None):
    t, d = x.shape
    d_in = win16.shape[1]
    const = lambda i: (0, 0)
    tile = lambda w: pl.BlockSpec((tm, w), lambda i: (i, 0))
    out_shape = [jax.ShapeDtypeStruct((t, D_A), BF16), jax.ShapeDtypeStruct((t, D_QK), BF16),
                 jax.ShapeDtypeStruct((t, D_QK), F32), jax.ShapeDtypeStruct((t, D_B), F32)]
    out_specs = [tile(D_A), tile(D_QK), tile(D_QK), tile(D_B)]
    if seq is not None:
        tiles = seq // tm
        out_shape += [jax.ShapeDtypeStruct((t, D_QK), BF16), jax.ShapeDtypeStruct((t // seq, D_B, seq), BF16)]
        out_specs += [tile(D_QK), pl.BlockSpec((1, D_B, tm), lambda i: (i // tiles, 0, i % tiles))]
    else:
        out_shape.append(jax.ShapeDtypeStruct((t, D_A), F32))
        out_specs.append(tile(D_A))
    return pl.pallas_call(
        functools.partial(_inproj_kernel, cs=cs, prompt=seq is not None),
        grid=(t // tm,),
        in_specs=[tile(d), pl.BlockSpec((1, d), const), pl.BlockSpec((d, d_in), const),
                  pl.BlockSpec((1, D_A), const), pl.BlockSpec((1, D_A), const),
                  pl.BlockSpec((G_A, cs, cs), lambda i: (0, 0, 0)), pl.BlockSpec((cs, D_A), const)],
        out_specs=out_specs,
        out_shape=out_shape,
        compiler_params=pltpu.CompilerParams(dimension_semantics=("arbitrary",),
                                             vmem_limit_bytes=VMEM_LIMIT),
        name="in_proj",
    )(x, g1, win16, lng, lnb, ws, bs_full)


def _attn_kernel(rb_ref, q_ref, k_ref, vt_ref, lam_ref, sg_ref, o_ref,
                 bias_scr, qzt_scr, m_scr, l_scr, acc_scr, *, t):
    b, h, qi = pl.program_id(0), pl.program_id(1), pl.program_id(2)

    @pl.when((b == 0) & (h == 0) & (qi == 0))
    def _build_bias_tiles():
        d0 = (lax.broadcasted_iota(jnp.int32, (t, t), 1) - lax.broadcasted_iota(jnp.int32, (t, t), 0))
        for hh in range(H_B):
            near = _bias_from_distance(jnp.maximum(d0, 0), lambda bk: rb_ref[bk, hh])
            near = jnp.where(d0 >= 0, near, NEG)
            prev = _bias_from_distance(d0 + t, lambda bk: rb_ref[bk, hh])
            bias_scr[hh, 0] = jnp.concatenate([near, near], axis=1)
            bias_scr[hh, 1] = jnp.concatenate([prev, prev], axis=1)

    qt = q_ref[0].astype(F32).T
    depth = lax.broadcasted_iota(jnp.int32, (LANES, t), 0)
    qzt_scr[...] = jnp.concatenate([jnp.where(depth < DH, qt, 0.0), jnp.where(depth >= DH, qt, 0.0)],
                                   axis=1).astype(BF16)

    def scores(kj, tile):
        k = k_ref[0, pl.ds(pl.multiple_of(kj * t, t), t), :]
        s = _dot(k, qzt_scr[...])
        return s if tile is None else s + bias_scr[h, tile]

    def pv(p, kj):
        vt = vt_ref[0, :, pl.ds(pl.multiple_of(kj * t, t), t)]
        return _dot(vt, p.astype(BF16))

    def update(s, shift, kj):
        m_old = m_scr[...]
        m_new = jnp.maximum(m_old, jnp.max(s, axis=0, keepdims=True) + shift)
        alpha = jnp.exp(m_old - m_new)
        p = jnp.exp(s - (m_new - shift))
        l_scr[...] = alpha * l_scr[...] + jnp.sum(p, axis=0, keepdims=True)
        acc_scr[...] = alpha * acc_scr[...] + pv(p, kj)
        m_scr[...] = m_new

    s = scores(qi, 0)
    m0 = jnp.max(s, axis=0, keepdims=True)
    p = jnp.exp(s - m0)
    m_scr[...] = m0
    l_scr[...] = jnp.sum(p, axis=0, keepdims=True)
    acc_scr[...] = pv(p, qi)

    @pl.when(qi > 0)
    def _previous_block():
        update(scores(qi - 1, 1), 0.0, qi - 1)

    far_bias = rb_ref[NUM_BUCKETS - 1, h]

    def far_body(kj, carry):
        update(scores(kj, None), far_bias, kj)
        return carry

    lax.fori_loop(0, jnp.maximum(qi - 1, 0), far_body, 0)

    ot = acc_scr[...] / l_scr[...]
    o = (ot[:, 0:t] - _lambda_full(lam_ref) * ot[:, t:2 * t]).T
    o_ref[0] = (_rms(o, sg_ref[...]) * (1.0 - LAMBDA_INIT)).astype(o_ref.dtype)


def _prompt_attention(rel_bias, q, k, vt, lam, sg, *, t):
    b, s, _ = q.shape
    const = lambda bi, hi, qi: (0, 0)
    return pl.pallas_call(
        functools.partial(_attn_kernel, t=t),
        grid=(b, H_B, s // t),
        in_specs=[pl.BlockSpec(memory_space=pltpu.SMEM),
                  pl.BlockSpec((1, t, LANES), lambda bi, hi, qi: (bi, qi, hi)),
                  pl.BlockSpec((1, s, LANES), lambda bi, hi, qi: (bi, 0, hi)),
                  pl.BlockSpec((1, DV, s), lambda bi, hi, qi: (bi, hi, 0)),
                  pl.BlockSpec((4, DH), const), pl.BlockSpec((1, DV), const)],
        out_specs=pl.BlockSpec((1, t, LANES), lambda bi, hi, qi: (bi, qi, hi)),
        out_shape=jax.ShapeDtypeStruct((b, s, D_B), BF16),
        scratch_shapes=[pltpu.VMEM((H_B, 2, t, 2 * t), F32), pltpu.VMEM((LANES, 2 * t), BF16),
                        pltpu.VMEM((1, 2 * t), F32), pltpu.VMEM((1, 2 * t), F32),
                        pltpu.VMEM((DV, 2 * t), F32)],
        compiler_params=pltpu.CompilerParams(dimension_semantics=("arbitrary",) * 3,
                                             vmem_limit_bytes=VMEM_LIMIT),
        name="prompt_attn",
    )(rel_bias, q, k, vt, lam, sg)


def _paged_kernel(pt_ref, rb_ref, q_ref, kn_ref, vn_ref, lam_ref, sg_ref, *rest, pages, dec):
    k_refs, v_refs = rest[:pages], rest[pages:2 * pages]
    o_ref, qh_scr, new_scr, tile_scr, far_scr, m_scr, l_scr, acc_scr = rest[2 * pages:]
    del pt_ref
    b, st = pl.program_id(0), pl.program_id(1)
    n_st = pl.num_programs(1)
    hrows = 2 * dec
    rows = H_B * hrows

    @pl.when((b == 0) & (st == 0))
    def _build_bias_tiles():
        r = lax.broadcasted_iota(jnp.int32, (rows, PAGE), 0)
        j = lax.broadcasted_iota(jnp.int32, (rows, PAGE), 1)
        head, i = r // hrows, r % dec

        def row_value(bk):
            val = jnp.zeros((rows, PAGE), F32) + rb_ref[bk, H_B - 1]
            for hh in range(H_B - 2, -1, -1):
                val = jnp.where(head == hh, rb_ref[bk, hh], val)
            return val

        tile_scr[0] = _bias_from_distance(PAGE + i - j, row_value)
        tile_scr[1] = jnp.where(j <= i, _bias_from_distance(jnp.maximum(i - j, 0), row_value), NEG)
        far_scr[...] = row_value(NUM_BUCKETS - 1)[:, 0:1]
        new_scr[...] = jnp.zeros_like(new_scr)

    @pl.when(st == 0)
    def _start_batch():
        r = lax.broadcasted_iota(jnp.int32, (hrows, LANES), 0)
        lane = lax.broadcasted_iota(jnp.int32, (hrows, LANES), 1)
        for hh in range(H_B):
            qh = q_ref[0, :, hh * LANES:(hh + 1) * LANES].astype(F32)
            qh = jnp.concatenate([qh, qh], axis=0)
            qh_scr[hh] = jnp.where(lane // DH == r // dec, qh, 0.0).astype(BF16)
        m_scr[...] = jnp.full_like(m_scr, -jnp.inf)
        l_scr[...] = jnp.zeros_like(l_scr)
        acc_scr[...] = jnp.zeros_like(acc_scr)

    def scores(head_keys):
        return jnp.concatenate([_dot_nt(qh_scr[hh], head_keys(hh)) for hh in range(H_B)], axis=0)

    def update(s, shift, head_values):
        m_old = m_scr[...]
        m_new = jnp.maximum(m_old, jnp.max(s, axis=-1, keepdims=True) + shift)
        alpha = jnp.exp(m_old - m_new)
        p = jnp.exp(s - (m_new - shift))
        l_scr[...] = alpha * l_scr[...] + jnp.sum(p, axis=-1, keepdims=True)
        p = p.astype(BF16)
        pv = jnp.concatenate([_dot(p[hh * hrows:(hh + 1) * hrows], head_values(hh))
                              for hh in range(H_B)], axis=0)
        acc_scr[...] = alpha * acc_scr[...] + pv
        m_scr[...] = m_new

    def cached(refs, first, last):
        return lambda hh: jnp.concatenate(
            [refs[p][0, pl.ds(hh, PAGE, stride=H_B), :] for p in range(first, last)], axis=0).astype(BF16)

    def fresh(which):
        return lambda hh: new_scr[which, :, hh * LANES:(hh + 1) * LANES].astype(BF16)

    @pl.when(st < n_st - 1)
    def _far_pages():
        update(scores(cached(k_refs, 0, pages)), far_scr[...], cached(v_refs, 0, pages))

    @pl.when(st == n_st - 1)
    def _last_pages_and_new_rows():
        if pages > 1:
            update(scores(cached(k_refs, 0, pages - 1)), far_scr[...], cached(v_refs, 0, pages - 1))
        update(scores(cached(k_refs, pages - 1, pages)) + tile_scr[0], 0.0,
               cached(v_refs, pages - 1, pages))
        new_scr[0, 0:dec, :] = kn_ref[0]
        new_scr[1, 0:dec, :] = vn_ref[0]
        update(scores(fresh(0)) + tile_scr[1], 0.0, fresh(1))

        o = acc_scr[...] / l_scr[...]
        lam = _lambda_full(lam_ref)
        for hh in range(H_B):
            r0 = hh * hrows
            oh = o[r0:r0 + dec, :] - lam * o[r0 + dec:r0 + hrows, :]
            o_ref[0, :, hh * DV:(hh + 1) * DV] = (_rms(oh, sg_ref[...])
                                                  * (1.0 - LAMBDA_INIT)).astype(o_ref.dtype)


def _sample_attention(page_table, rel_bias, q, k_new, v_new, cache_k, cache_v, lam, sg, *, pages):
    nb, dec, _ = q.shape
    n_pages = page_table.shape[1]
    n_st = n_pages // pages
    rows = 2 * H_B * dec
    const2 = lambda bi, si, pt: (0, 0)
    per_batch = pl.BlockSpec((1, dec, D_QK), lambda bi, si, pt: (bi, 0, 0))

    def page_spec(p):
        return pl.BlockSpec((1, PAGE * H_B, LANES),
                            lambda bi, si, pt: (pt[bi * n_pages + si * pages + p], 0, 0))

    grid_spec = pltpu.PrefetchScalarGridSpec(
        num_scalar_prefetch=1,
        grid=(nb, n_st),
        in_specs=[pl.BlockSpec(memory_space=pltpu.SMEM), per_batch, per_batch, per_batch,
                  pl.BlockSpec((4, DH), const2), pl.BlockSpec((1, DV), const2)]
                 + [page_spec(p) for p in range(pages)] * 2,
        out_specs=per_batch,
        scratch_shapes=[pltpu.VMEM((H_B, 2 * dec, LANES), BF16), pltpu.VMEM((2, PAGE, D_QK), F32),
                        pltpu.VMEM((2, rows, PAGE), F32), pltpu.VMEM((rows, 1), F32),
                        pltpu.VMEM((rows, 1), F32), pltpu.VMEM((rows, 1), F32),
                        pltpu.VMEM((rows, DV), F32)],
    )
    return pl.pallas_call(
        functools.partial(_paged_kernel, pages=pages, dec=dec),
        grid_spec=grid_spec,
        out_shape=jax.ShapeDtypeStruct((nb, dec, D_B), BF16),
        compiler_params=pltpu.CompilerParams(dimension_semantics=("arbitrary", "arbitrary"),
                                             vmem_limit_bytes=VMEM_LIMIT),
        name="paged_attn",
    )(page_table.reshape(-1), rel_bias, q, k_new, v_new, lam, sg,
      *([cache_k] * pages), *([cache_v] * pages))


def _tail_kernel(x_ref, a_ref, b_ref, wo_ref, g2_ref, wg_ref, wu_ref, wd_ref, gf_ref, y_ref):
    h = x_ref[...] + _dot(a_ref[...], wo_ref[0:D_A, :]) + _dot(b_ref[...], wo_ref[D_A:, :])
    hn = _rms(h, g2_ref[...]).astype(BF16)
    gate = _dot(hn, wg_ref[...])
    f = (gate * (1.0 / (1.0 + jnp.exp(-gate))) * _dot(hn, wu_ref[...])).astype(BF16)
    y_ref[...] = _rms(h + _dot(f, wd_ref[...]), gf_ref[...])


def _tail(x, a, bmix, wo16, g2, wg16, wu16, wd16, gf, *, tm):
    t, d = x.shape
    d_ff = wg16.shape[1]
    const = lambda i: (0, 0)
    resident = lambda shape: pl.BlockSpec(shape, const, pipeline_mode=pl.Buffered(1))
    return pl.pallas_call(
        _tail_kernel,
        grid=(t // tm,),
        in_specs=[pl.BlockSpec((tm, d), lambda i: (i, 0)), pl.BlockSpec((tm, D_A), lambda i: (i, 0)),
                  pl.BlockSpec((tm, D_B), lambda i: (i, 0)), resident((D_A + D_B, d)),
                  pl.BlockSpec((1, d), const), resident((d, d_ff)), resident((d, d_ff)),
                  resident((d_ff, d)), pl.BlockSpec((1, d), const)],
        out_specs=pl.BlockSpec((tm, d), lambda i: (i, 0)),
        out_shape=jax.ShapeDtypeStruct((t, d), F32),
        compiler_params=pltpu.CompilerParams(dimension_semantics=("arbitrary",),
                                             vmem_limit_bytes=VMEM_LIMIT),
        name="tail",
    )(x, a, bmix, wo16, g2, wg16, wu16, wd16, gf)


def kernel(x_prompt, x_sample, cache_k, cache_v, page_table, norm1_g, w_in, gmlp_ln_g, gmlp_ln_b, gmlp_ws, gmlp_bs, lambda_q1, lambda_k1, lambda_q2, lambda_k2, subln_g, rel_bias, w_out, norm2_g, w_gate, w_up, w_down, final_g):
    nb, seq, d = x_prompt.shape
    db, dec, _ = x_sample.shape
    row = lambda p: p.reshape(1, -1)

    win16, wo16 = w_in.astype(BF16), w_out.astype(BF16)
    wg16, wu16, wd16 = w_gate.astype(BF16), w_up.astype(BF16), w_down.astype(BF16)
    lam = jnp.stack([lambda_q1, lambda_k1, lambda_q2, lambda_k2])
    proj_params = (row(norm1_g), win16, row(gmlp_ln_g), row(gmlp_ln_b))
    tail_params = (wo16, row(norm2_g), wg16, wu16, wd16, row(final_g))

    xp = x_prompt.reshape(nb * seq, d)
    bs_p = jnp.repeat(gmlp_bs.T, CH_A, axis=1)
    a_p, q_p, kf_p, vf_p, kb_p, vt_p = _in_proj(xp, *proj_params, gmlp_ws, bs_p,
                                               tm=512, cs=CHUNK, seq=seq)
    shape3 = lambda z: z.reshape(nb, seq, -1)
    b_p = _prompt_attention(rel_bias, shape3(q_p), shape3(kb_p), vt_p, lam, row(subln_g), t=256)
    y_p = _tail(xp, a_p, b_p.reshape(nb * seq, D_B), *tail_params, tm=512)

    xs = x_sample.reshape(db * dec, d)
    ws_s = jnp.einsum('ab,gts->gatbs', jnp.eye(db, dtype=F32),
                      gmlp_ws[:, :dec, :dec]).reshape(G_A, db * dec, db * dec)
    bs_s = jnp.tile(jnp.repeat(gmlp_bs[:, :dec].T, CH_A, axis=1), (db, 1))
    a_s, q_s, kf_s, vf_s, gv_s = _in_proj(xs, *proj_params, ws_s, bs_s, tm=db * dec, cs=db * dec)
    b_s = _sample_attention(page_table, rel_bias, q_s.reshape(db, dec, D_QK),
                            kf_s.reshape(db, dec, D_QK), vf_s.reshape(db, dec, D_B),
                            cache_k.reshape(-1, PAGE * H_B, DV), cache_v.reshape(-1, PAGE * H_B, DV),
                            lam, row(subln_g), pages=8)
    y_s = _tail(xs, a_s, b_s.reshape(db * dec, D_B), *tail_params, tm=db * dec)

    return (y_p.reshape(nb, seq, d), y_s.reshape(db, dec, d),
            kf_p.reshape(nb, seq, H_B, 2 * DH), vf_p.reshape(nb, seq, H_B, DV),
            kf_s.reshape(db, dec, H_B, 2 * DH), vf_s.reshape(db, dec, H_B, DV),
            gv_s.reshape(db, dec, D_A))
```

```python
import functools
import math

import numpy as np
import jax
import jax.numpy as jnp
from jax import lax
from jax.experimental import pallas as pl
from jax.experimental.pallas import tpu as pltpu

F32 = jnp.float32
BF16 = jnp.bfloat16

EPS = 1e-6
LANES = 128
G_A = 4
CH_A = 128
CHUNK = 128
H_B = 4
DV = 128
DH = 64
D_A = G_A * CH_A
D_QK = H_B * 2 * DH
D_B = H_B * DV
SCALE = DH ** -0.5
LOG2E = math.log2(math.e)
LAMBDA_INIT = 0.8 - 0.6 * math.exp(-0.3 * 0)
NUM_BUCKETS = 32
MAX_EXACT = NUM_BUCKETS // 2
MAX_DISTANCE = 128
PAGE = 128
NEG = -1e30
VMEM_LIMIT = 56 * 1024 * 1024


def _bucket_thresholds():
    n = np.arange(0, 2 * MAX_DISTANCE)
    nf = np.maximum(n, 1).astype(np.float64)
    large = MAX_EXACT + (np.log(nf / MAX_EXACT) / math.log(MAX_DISTANCE / MAX_EXACT)
                         * (NUM_BUCKETS - MAX_EXACT)).astype(np.int32)
    bucket = np.where(n < MAX_EXACT, n, np.minimum(large, NUM_BUCKETS - 1))
    assert np.all(np.diff(bucket) >= 0) and bucket[MAX_DISTANCE - 1] == NUM_BUCKETS - 1
    return [int(n[bucket >= b].min()) for b in range(NUM_BUCKETS)]


BUCKET_START = _bucket_thresholds()


def _rms(x, g):
    ms = jnp.mean(x * x, axis=-1, keepdims=True)
    return x * lax.rsqrt(ms + EPS) * g


def _dot(a, b):
    return jnp.dot(a, b, preferred_element_type=F32)


def _dot_nt(a, b):
    return lax.dot_general(a, b, (((1,), (1,)), ((), ())), preferred_element_type=F32)


def _bias_from_distance(n, value_of_bucket):
    val = jnp.zeros(n.shape, F32) + value_of_bucket(0)
    for b in range(1, NUM_BUCKETS):
        val = jnp.where(n >= BUCKET_START[b], value_of_bucket(b), val)
    return val


def _lambda_full(lam_ref):
    s1 = jnp.sum(lam_ref[0:1, :] * lam_ref[1:2, :], axis=-1, keepdims=True)
    s2 = jnp.sum(lam_ref[2:3, :] * lam_ref[3:4, :], axis=-1, keepdims=True)
    return jnp.exp(s1) - jnp.exp(s2) + LAMBDA_INIT


def _inproj_kernel(x_ref, g1_ref, win_ref, lng_ref, lnb_ref, ws_ref, bs_ref, *out_refs, cs, prompt):
    if prompt:
        a_ref, q_ref, kf_ref, vf_ref, kb_ref, vt_ref = out_refs
    else:
        a_ref, q_ref, kf_ref, vf_ref, vn_ref = out_refs
    tm = x_ref.shape[0]
    xn = _rms(x_ref[...], g1_ref[...]).astype(BF16)

    q = _dot(xn, win_ref[:, 2 * D_A:2 * D_A + D_QK])
    q_ref[...] = (q * (SCALE * LOG2E if prompt else SCALE)).astype(BF16)
    k = _dot(xn, win_ref[:, 2 * D_A + D_QK:2 * D_A + 2 * D_QK])
    vb = _dot(xn, win_ref[:, 2 * D_A + 2 * D_QK:])
    for hh in range(H_B):
        kf_ref[pl.ds(hh, tm, stride=H_B), :] = k[:, hh * LANES:(hh + 1) * LANES]
        vf_ref[pl.ds(hh, tm, stride=H_B), :] = vb[:, hh * LANES:(hh + 1) * LANES]
    if prompt:
        kb_ref[...] = k.astype(BF16)
        vt_ref[0] = vb.T.astype(BF16)

    u = _dot(xn, win_ref[:, 0:D_A])
    v = _dot(xn, win_ref[:, D_A:2 * D_A])
    row = lax.broadcasted_iota(jnp.int32, (cs, cs), 0)
    col = lax.broadcasted_iota(jnp.int32, (cs, cs), 1)
    for g in range(G_A):
        gs = slice(g * CH_A, (g + 1) * CH_A)
        vg = v[:, gs]
        mu = jnp.mean(vg, axis=-1, keepdims=True)
        d = vg - mu
        var = jnp.mean(d * d, axis=-1, keepdims=True)
        vn = d * lax.rsqrt(var + EPS) * lng_ref[:, gs] + lnb_ref[:, gs]
        if not prompt:
            vn_ref[:, gs] = vn
        vn16 = vn.astype(BF16)
        w = jnp.where(row >= col, ws_ref[g], 0.0).astype(BF16)
        for c in range(tm // cs):
            rs = slice(c * cs, (c + 1) * cs)
            s = _dot(w, vn16[rs, :]) + bs_ref[:, gs]
            a_ref[rs, gs] = (u[rs, gs] * s).astype(BF16)


def _in_proj(x, g1, win16, lng, lnb, ws, bs_full, *, tm, cs, seq=None):
    t, d = x.shape
    d_in = win16.shape[1]
    const = lambda i: (0, 0)
    tile = lambda w: pl.BlockSpec((tm, w), lambda i: (i, 0))
    heads = pl.BlockSpec((tm * H_B, LANES), lambda i: (i, 0))
    out_shape = [jax.ShapeDtypeStruct((t, D_A), BF16), jax.ShapeDtypeStruct((t, D_QK), BF16),
                 jax.ShapeDtypeStruct((t * H_B, LANES), F32), jax.ShapeDtypeStruct((t * H_B, LANES), F32)]
    out_specs = [tile(D_A), tile(D_QK), heads, heads]
    if seq is not None:
        tiles = seq // tm
        out_shape += [jax.ShapeDtypeStruct((t, D_QK), BF16), jax.ShapeDtypeStruct((t // seq, D_B, seq), BF16)]
        out_specs += [tile(D_QK), pl.BlockSpec((1, D_B, tm), lambda i: (i // tiles, 0, i % tiles))]
    else:
        out_shape.append(jax.ShapeDtypeStruct((t, D_A), F32))
        out_specs.append(tile(D_A))
    return pl.pallas_call(
        functools.partial(_inproj_kernel, cs=cs, prompt=seq is not None),
        grid=(t // tm,),
        in_specs=[tile(d), pl.BlockSpec((1, d), const), pl.BlockSpec((d, d_in), const),
                  pl.BlockSpec((1, D_A), const), pl.BlockSpec((1, D_A), const),
                  pl.BlockSpec((G_A, cs, cs), lambda i: (0, 0, 0)), pl.BlockSpec((cs, D_A), const)],
        out_specs=out_specs,
        out_shape=out_shape,
        compiler_params=pltpu.CompilerParams(dimension_semantics=("arbitrary",),
                                             vmem_limit_bytes=VMEM_LIMIT),
        name="in_proj",
    )(x, g1, win16, lng, lnb, ws, bs_full)


def _attn_kernel(rb_ref, q_ref, k_ref, vt_ref, lam_ref, sg_ref, o_ref,
                 bias_scr, qzt_scr, m_scr, l_scr, acc_scr, *, t):
    b, h, qi = pl.program_id(0), pl.program_id(1), pl.program_id(2)

    @pl.when((b == 0) & (h == 0) & (qi == 0))
    def _build_bias_tiles():
        d0 = (lax.broadcasted_iota(jnp.int32, (t, t), 1) - lax.broadcasted_iota(jnp.int32, (t, t), 0))
        for hh in range(H_B):
            near = _bias_from_distance(jnp.maximum(d0, 0), lambda bk: rb_ref[bk, hh]) * LOG2E
            bias_scr[hh, 0] = jnp.where(d0 >= 0, near, NEG)
            bias_scr[hh, 1] = _bias_from_distance(d0 + t, lambda bk: rb_ref[bk, hh]) * LOG2E

    qt = q_ref[0].astype(F32).T
    depth = lax.broadcasted_iota(jnp.int32, (LANES, t), 0)
    qzt_scr[...] = jnp.concatenate([jnp.where(depth < DH, qt, 0.0), jnp.where(depth >= DH, qt, 0.0)],
                                   axis=1).astype(BF16)

    def scores(kj, tile):
        k = k_ref[0, pl.ds(pl.multiple_of(kj * t, t), t), :]
        s = _dot(k, qzt_scr[...])
        if tile is None:
            return s
        bias = bias_scr[h, tile]
        return jnp.concatenate([s[:, 0:t] + bias, s[:, t:2 * t] + bias], axis=1)

    def pv(p, kj):
        vt = vt_ref[0, :, pl.ds(pl.multiple_of(kj * t, t), t)]
        return _dot(vt, p.astype(BF16))

    def update(s, shift, kj):
        m_old = m_scr[...]
        m_new = jnp.maximum(m_old, jnp.max(s, axis=0, keepdims=True) + shift)
        alpha = jnp.exp2(m_old - m_new)
        p = jnp.exp2(s - (m_new - shift))
        l_scr[...] = alpha * l_scr[...] + jnp.sum(p, axis=0, keepdims=True)
        acc_scr[...] = alpha * acc_scr[...] + pv(p, kj)
        m_scr[...] = m_new

    s = scores(qi, 0)
    m0 = jnp.max(s, axis=0, keepdims=True)
    p = jnp.exp2(s - m0)
    m_scr[...] = m0
    l_scr[...] = jnp.sum(p, axis=0, keepdims=True)
    acc_scr[...] = pv(p, qi)

    @pl.when(qi > 0)
    def _previous_block():
        update(scores(qi - 1, 1), 0.0, qi - 1)

    far_bias = rb_ref[NUM_BUCKETS - 1, h] * LOG2E

    def far_body(kj, carry):
        update(scores(kj, None), far_bias, kj)
        return carry

    lax.fori_loop(0, jnp.maximum(qi - 1, 0), far_body, 0)

    ot = acc_scr[...] / l_scr[...]
    o = (ot[:, 0:t] - _lambda_full(lam_ref) * ot[:, t:2 * t]).T
    o_ref[0] = (_rms(o, sg_ref[...]) * (1.0 - LAMBDA_INIT)).astype(o_ref.dtype)


def _prompt_attention(rel_bias, q, k, vt, lam, sg, *, t):
    b, s, _ = q.shape
    const = lambda bi, hi, qi: (0, 0)
    return pl.pallas_call(
        functools.partial(_attn_kernel, t=t),
        grid=(b, H_B, s // t),
        in_specs=[pl.BlockSpec(memory_space=pltpu.SMEM),
                  pl.BlockSpec((1, t, LANES), lambda bi, hi, qi: (bi, qi, hi)),
                  pl.BlockSpec((1, s, LANES), lambda bi, hi, qi: (bi, 0, hi)),
                  pl.BlockSpec((1, DV, s), lambda bi, hi, qi: (bi, hi, 0)),
                  pl.BlockSpec((4, DH), const), pl.BlockSpec((1, DV), const)],
        out_specs=pl.BlockSpec((1, t, LANES), lambda bi, hi, qi: (bi, qi, hi)),
        out_shape=jax.ShapeDtypeStruct((b, s, D_B), BF16),
        scratch_shapes=[pltpu.VMEM((H_B, 2, t, t), F32), pltpu.VMEM((LANES, 2 * t), BF16),
                        pltpu.VMEM((1, 2 * t), F32), pltpu.VMEM((1, 2 * t), F32),
                        pltpu.VMEM((DV, 2 * t), F32)],
        compiler_params=pltpu.CompilerParams(dimension_semantics=("arbitrary",) * 3,
                                             vmem_limit_bytes=VMEM_LIMIT),
        name="prompt_attn",
    )(rel_bias, q, k, vt, lam, sg)


def _paged_kernel(pt_ref, rb_ref, q_ref, kn_ref, vn_ref, lam_ref, sg_ref, *rest, pages, dec):
    k_refs, v_refs = rest[:pages], rest[pages:2 * pages]
    o_ref, qh_scr, new_scr, tile_scr, far_scr, m_scr, l_scr, acc_scr = rest[2 * pages:]
    del pt_ref
    b, st = pl.program_id(0), pl.program_id(1)
    n_st = pl.num_programs(1)
    hrows = 2 * dec
    rows = H_B * hrows

    @pl.when((b == 0) & (st == 0))
    def _build_bias_tiles():
        r = lax.broadcasted_iota(jnp.int32, (rows, PAGE), 0)
        j = lax.broadcasted_iota(jnp.int32, (rows, PAGE), 1)
        head, i = r // hrows, r % dec

        def row_value(bk):
            val = jnp.zeros((rows, PAGE), F32) + rb_ref[bk, H_B - 1]
            for hh in range(H_B - 2, -1, -1):
                val = jnp.where(head == hh, rb_ref[bk, hh], val)
            return val

        tile_scr[0] = _bias_from_distance(PAGE + i - j, row_value)
        tile_scr[1] = jnp.where(j <= i, _bias_from_distance(jnp.maximum(i - j, 0), row_value), NEG)
        far_scr[...] = row_value(NUM_BUCKETS - 1)[:, 0:1]
        new_scr[...] = jnp.zeros_like(new_scr)

    @pl.when(st == 0)
    def _start_batch():
        r = lax.broadcasted_iota(jnp.int32, (hrows, LANES), 0)
        lane = lax.broadcasted_iota(jnp.int32, (hrows, LANES), 1)
        for hh in range(H_B):
            qh = q_ref[0, :, hh * LANES:(hh + 1) * LANES].astype(F32)
            qh = jnp.concatenate([qh, qh], axis=0)
            qh_scr[hh] = jnp.where(lane // DH == r // dec, qh, 0.0).astype(BF16)
        m_scr[...] = jnp.full_like(m_scr, -jnp.inf)
        l_scr[...] = jnp.zeros_like(l_scr)
        acc_scr[...] = jnp.zeros_like(acc_scr)

    def scores(head_keys):
        return jnp.concatenate([_dot_nt(qh_scr[hh], head_keys(hh)) for hh in range(H_B)], axis=0)

    def update(s, shift, head_values):
        m_old = m_scr[...]
        m_new = jnp.maximum(m_old, jnp.max(s, axis=-1, keepdims=True) + shift)
        alpha = jnp.exp(m_old - m_new)
        p = jnp.exp(s - (m_new - shift))
        l_scr[...] = alpha * l_scr[...] + jnp.sum(p, axis=-1, keepdims=True)
        p = p.astype(BF16)
        pv = jnp.concatenate([_dot(p[hh * hrows:(hh + 1) * hrows], head_values(hh))
                              for hh in range(H_B)], axis=0)
        acc_scr[...] = alpha * acc_scr[...] + pv
        m_scr[...] = m_new

    def cached(refs, first, last):
        return lambda hh: jnp.concatenate(
            [refs[p][0, pl.ds(hh, PAGE, stride=H_B), :] for p in range(first, last)], axis=0).astype(BF16)

    def fresh(which):
        return lambda hh: new_scr[which, :, hh * LANES:(hh + 1) * LANES].astype(BF16)

    @pl.when(st < n_st - 1)
    def _far_pages():
        update(scores(cached(k_refs, 0, pages)), far_scr[...], cached(v_refs, 0, pages))

    @pl.when(st == n_st - 1)
    def _last_pages_and_new_rows():
        if pages > 1:
            update(scores(cached(k_refs, 0, pages - 1)), far_scr[...], cached(v_refs, 0, pages - 1))
        update(scores(cached(k_refs, pages - 1, pages)) + tile_scr[0], 0.0,
               cached(v_refs, pages - 1, pages))
        new_scr[0, 0:dec, :] = kn_ref[0]
        new_scr[1, 0:dec, :] = vn_ref[0]
        update(scores(fresh(0)) + tile_scr[1], 0.0, fresh(1))

        o = acc_scr[...] / l_scr[...]
        lam = _lambda_full(lam_ref)
        for hh in range(H_B):
            r0 = hh * hrows
            oh = o[r0:r0 + dec, :] - lam * o[r0 + dec:r0 + hrows, :]
            o_ref[0, :, hh * DV:(hh + 1) * DV] = (_rms(oh, sg_ref[...])
                                                  * (1.0 - LAMBDA_INIT)).astype(o_ref.dtype)


def _sample_attention(page_table, rel_bias, q, k_new, v_new, cache_k, cache_v, lam, sg, *, pages):
    nb, dec, _ = q.shape
    n_pages = page_table.shape[1]
    n_st = n_pages // pages
    rows = 2 * H_B * dec
    const2 = lambda bi, si, pt: (0, 0)
    per_batch = pl.BlockSpec((1, dec, D_QK), lambda bi, si, pt: (bi, 0, 0))

    def page_spec(p):
        return pl.BlockSpec((1, PAGE * H_B, LANES),
                            lambda bi, si, pt: (pt[bi * n_pages + si * pages + p], 0, 0))

    grid_spec = pltpu.PrefetchScalarGridSpec(
        num_scalar_prefetch=1,
        grid=(nb, n_st),
        in_specs=[pl.BlockSpec(memory_space=pltpu.SMEM), per_batch, per_batch, per_batch,
                  pl.BlockSpec((4, DH), const2), pl.BlockSpec((1, DV), const2)]
                 + [page_spec(p) for p in range(pages)] * 2,
        out_specs=per_batch,
        scratch_shapes=[pltpu.VMEM((H_B, 2 * dec, LANES), BF16), pltpu.VMEM((2, PAGE, D_QK), F32),
                        pltpu.VMEM((2, rows, PAGE), F32), pltpu.VMEM((rows, 1), F32),
                        pltpu.VMEM((rows, 1), F32), pltpu.VMEM((rows, 1), F32),
                        pltpu.VMEM((rows, DV), F32)],
    )
    return pl.pallas_call(
        functools.partial(_paged_kernel, pages=pages, dec=dec),
        grid_spec=grid_spec,
        out_shape=jax.ShapeDtypeStruct((nb, dec, D_B), BF16),
        compiler_params=pltpu.CompilerParams(dimension_semantics=("arbitrary", "arbitrary"),
                                             vmem_limit_bytes=VMEM_LIMIT),
        name="paged_attn",
    )(page_table.reshape(-1), rel_bias, q, k_new, v_new, lam, sg,
      *([cache_k] * pages), *([cache_v] * pages))


def _tail_kernel(x_ref, a_ref, b_ref, wo_ref, g2_ref, wg_ref, wu_ref, wd_ref, gf_ref, y_ref):
    h = x_ref[...] + _dot(a_ref[...], wo_ref[0:D_A, :]) + _dot(b_ref[...], wo_ref[D_A:, :])
    hn = _rms(h, g2_ref[...]).astype(BF16)
    gate = _dot(hn, wg_ref[...])
    f = (gate * (1.0 / (1.0 + jnp.exp(-gate))) * _dot(hn, wu_ref[...])).astype(BF16)
    y_ref[...] = _rms(h + _dot(f, wd_ref[...]), gf_ref[...])


def _tail(x, a, bmix, wo16, g2, wg16, wu16, wd16, gf, *, tm):
    t, d = x.shape
    d_ff = wg16.shape[1]
    const = lambda i: (0, 0)
    resident = lambda shape: pl.BlockSpec(shape, const, pipeline_mode=pl.Buffered(1))
    return pl.pallas_call(
        _tail_kernel,
        grid=(t // tm,),
        in_specs=[pl.BlockSpec((tm, d), lambda i: (i, 0)), pl.BlockSpec((tm, D_A), lambda i: (i, 0)),
                  pl.BlockSpec((tm, D_B), lambda i: (i, 0)), resident((D_A + D_B, d)),
                  pl.BlockSpec((1, d), const), resident((d, d_ff)), resident((d, d_ff)),
                  resident((d_ff, d)), pl.BlockSpec((1, d), const)],
        out_specs=pl.BlockSpec((tm, d), lambda i: (i, 0)),
        out_shape=jax.ShapeDtypeStruct((t, d), F32),
        compiler_params=pltpu.CompilerParams(dimension_semantics=("arbitrary",),
                                             vmem_limit_bytes=VMEM_LIMIT),
        name="tail",
    )(x, a, bmix, wo16, g2, wg16, wu16, wd16, gf)


def kernel(x_prompt, x_sample, cache_k, cache_v, page_table, norm1_g, w_in, gmlp_ln_g, gmlp_ln_b, gmlp_ws, gmlp_bs, lambda_q1, lambda_k1, lambda_q2, lambda_k2, subln_g, rel_bias, w_out, norm2_g, w_gate, w_up, w_down, final_g):
    nb, seq, d = x_prompt.shape
    db, dec, _ = x_sample.shape
    row = lambda p: p.reshape(1, -1)

    win16, wo16 = w_in.astype(BF16), w_out.astype(BF16)
    wg16, wu16, wd16 = w_gate.astype(BF16), w_up.astype(BF16), w_down.astype(BF16)
    lam = jnp.stack([lambda_q1, lambda_k1, lambda_q2, lambda_k2])
    proj_params = (row(norm1_g), win16, row(gmlp_ln_g), row(gmlp_ln_b))
    tail_params = (wo16, row(norm2_g), wg16, wu16, wd16, row(final_g))

    xp = x_prompt.reshape(nb * seq, d)
    bs_p = jnp.repeat(gmlp_bs.T, CH_A, axis=1)
    a_p, q_p, kf_p, vf_p, kb_p, vt_p = _in_proj(xp, *proj_params, gmlp_ws, bs_p,
                                               tm=512, cs=CHUNK, seq=seq)
    shape3 = lambda z: z.reshape(nb, seq, -1)
    b_p = _prompt_attention(rel_bias, shape3(q_p), shape3(kb_p), vt_p, lam, row(subln_g), t=512)
    y_p = _tail(xp, a_p, b_p.reshape(nb * seq, D_B), *tail_params, tm=512)

    xs = x_sample.reshape(db * dec, d)
    ws_s = jnp.einsum('ab,gts->gatbs', jnp.eye(db, dtype=F32),
                      gmlp_ws[:, :dec, :dec]).reshape(G_A, db * dec, db * dec)
    bs_s = jnp.tile(jnp.repeat(gmlp_bs[:, :dec].T, CH_A, axis=1), (db, 1))
    a_s, q_s, kf_s, vf_s, gv_s = _in_proj(xs, *proj_params, ws_s, bs_s, tm=db * dec, cs=db * dec)
    b_s = _sample_attention(page_table, rel_bias, q_s.reshape(db, dec, D_QK),
                            kf_s.reshape(db, dec, D_QK), vf_s.reshape(db, dec, D_B),
                            cache_k.reshape(-1, PAGE * H_B, DV), cache_v.reshape(-1, PAGE * H_B, DV),
                            lam, row(subln_g), pages=16)
    y_s = _tail(xs, a_s, b_s.reshape(db * dec, D_B), *tail_params, tm=db * dec)

    return (y_p.reshape(nb, seq, d), y_s.reshape(db, dec, d),
            kf_p.reshape(nb, seq, H_B, 2 * DH), vf_p.reshape(nb, seq, H_B, DV),
            kf_s.reshape(db, dec, H_B, 2 * DH), vf_s.reshape(db, dec, H_B, DV),
            gv_s.reshape(db, dec, D_A))
```

```python
import functools
import math

import numpy as np
import jax
import jax.numpy as jnp
from jax import lax
from jax.experimental import pallas as pl
from jax.experimental.pallas import tpu as pltpu

F32 = jnp.float32
BF16 = jnp.bfloat16

EPS = 1e-6
LANES = 128
G_A = 4
CH_A = 128
CHUNK = 128
H_B = 4
DV = 128
DH = 64
D_A = G_A * CH_A
D_QK = H_B * 2 * DH
D_B = H_B * DV
SCALE = DH ** -0.5
LOG2E = math.log2(math.e)
LAMBDA_INIT = 0.8 - 0.6 * math.exp(-0.3 * 0)
NUM_BUCKETS = 32
MAX_EXACT = NUM_BUCKETS // 2
MAX_DISTANCE = 128
PAGE = 128
ONES = 16
NEG = -1e30
VMEM_LIMIT = 56 * 1024 * 1024


def _bucket_thresholds():
    n = np.arange(0, 2 * MAX_DISTANCE)
    nf = np.maximum(n, 1).astype(np.float64)
    large = MAX_EXACT + (np.log(nf / MAX_EXACT) / math.log(MAX_DISTANCE / MAX_EXACT)
                         * (NUM_BUCKETS - MAX_EXACT)).astype(np.int32)
    bucket = np.where(n < MAX_EXACT, n, np.minimum(large, NUM_BUCKETS - 1))
    assert np.all(np.diff(bucket) >= 0) and bucket[MAX_DISTANCE - 1] == NUM_BUCKETS - 1
    return [int(n[bucket >= b].min()) for b in range(NUM_BUCKETS)]


BUCKET_START = _bucket_thresholds()


def _rms(x, g):
    ms = jnp.mean(x * x, axis=-1, keepdims=True)
    return x * lax.rsqrt(ms + EPS) * g


def _dot(a, b):
    return jnp.dot(a, b, preferred_element_type=F32)


def _dot_nt(a, b):
    return lax.dot_general(a, b, (((1,), (1,)), ((), ())), preferred_element_type=F32)


def _bias_from_distance(n, value_of_bucket):
    val = jnp.zeros(n.shape, F32) + value_of_bucket(0)
    for b in range(1, NUM_BUCKETS):
        val = jnp.where(n >= BUCKET_START[b], value_of_bucket(b), val)
    return val


def _lambda_full(lam_ref):
    s1 = jnp.sum(lam_ref[0:1, :] * lam_ref[1:2, :], axis=-1, keepdims=True)
    s2 = jnp.sum(lam_ref[2:3, :] * lam_ref[3:4, :], axis=-1, keepdims=True)
    return jnp.exp(s1) - jnp.exp(s2) + LAMBDA_INIT


def _inproj_kernel(x_ref, g1_ref, win_ref, lng_ref, lnb_ref, ws_ref, bs_ref, *out_refs, cs, prompt):
    if prompt:
        a_ref, q_ref, kf_ref, vf_ref, kb_ref, vt_ref = out_refs
    else:
        a_ref, q_ref, kf_ref, vf_ref, vn_ref = out_refs
    tm = x_ref.shape[0]
    xn = _rms(x_ref[...], g1_ref[...]).astype(BF16)

    q = _dot(xn, win_ref[:, 2 * D_A:2 * D_A + D_QK])
    q_ref[...] = (q * (SCALE * LOG2E if prompt else SCALE)).astype(BF16)
    k = _dot(xn, win_ref[:, 2 * D_A + D_QK:2 * D_A + 2 * D_QK])
    vb = _dot(xn, win_ref[:, 2 * D_A + 2 * D_QK:])
    for hh in range(H_B):
        kf_ref[pl.ds(hh, tm, stride=H_B), :] = k[:, hh * LANES:(hh + 1) * LANES]
        vf_ref[pl.ds(hh, tm, stride=H_B), :] = vb[:, hh * LANES:(hh + 1) * LANES]
    if prompt:
        kb_ref[...] = k.astype(BF16)
        vt_ref[0] = vb.T.astype(BF16)

    u = _dot(xn, win_ref[:, 0:D_A])
    v = _dot(xn, win_ref[:, D_A:2 * D_A])
    row = lax.broadcasted_iota(jnp.int32, (cs, cs), 0)
    col = lax.broadcasted_iota(jnp.int32, (cs, cs), 1)
    for g in range(G_A):
        gs = slice(g * CH_A, (g + 1) * CH_A)
        vg = v[:, gs]
        mu = jnp.mean(vg, axis=-1, keepdims=True)
        d = vg - mu
        var = jnp.mean(d * d, axis=-1, keepdims=True)
        vn = d * lax.rsqrt(var + EPS) * lng_ref[:, gs] + lnb_ref[:, gs]
        if not prompt:
            vn_ref[:, gs] = vn
        vn16 = vn.astype(BF16)
        w = jnp.where(row >= col, ws_ref[g], 0.0).astype(BF16)
        for c in range(tm // cs):
            rs = slice(c * cs, (c + 1) * cs)
            s = _dot(w, vn16[rs, :]) + bs_ref[:, gs]
            a_ref[rs, gs] = (u[rs, gs] * s).astype(BF16)


def _in_proj(x, g1, win16, lng, lnb, ws, bs_full, *, tm, cs, seq=None):
    t, d = x.shape
    d_in = win16.shape[1]
    const = lambda i: (0, 0)
    tile = lambda w: pl.BlockSpec((tm, w), lambda i: (i, 0))
    heads = pl.BlockSpec((tm * H_B, LANES), lambda i: (i, 0))
    out_shape = [jax.ShapeDtypeStruct((t, D_A), BF16), jax.ShapeDtypeStruct((t, D_QK), BF16),
                 jax.ShapeDtypeStruct((t * H_B, LANES), F32), jax.ShapeDtypeStruct((t * H_B, LANES), F32)]
    out_specs = [tile(D_A), tile(D_QK), heads, heads]
    if seq is not None:
        tiles = seq // tm
        out_shape += [jax.ShapeDtypeStruct((t, D_QK), BF16), jax.ShapeDtypeStruct((t // seq, D_B, seq), BF16)]
        out_specs += [tile(D_QK), pl.BlockSpec((1, D_B, tm), lambda i: (i // tiles, 0, i % tiles))]
    else:
        out_shape.append(jax.ShapeDtypeStruct((t, D_A), F32))
        out_specs.append(tile(D_A))
    return pl.pallas_call(
        functools.partial(_inproj_kernel, cs=cs, prompt=seq is not None),
        grid=(t // tm,),
        in_specs=[tile(d), pl.BlockSpec((1, d), const), pl.BlockSpec((d, d_in), const),
                  pl.BlockSpec((1, D_A), const), pl.BlockSpec((1, D_A), const),
                  pl.BlockSpec((G_A, cs, cs), lambda i: (0, 0, 0)), pl.BlockSpec((cs, D_A), const)],
        out_specs=out_specs,
        out_shape=out_shape,
        compiler_params=pltpu.CompilerParams(dimension_semantics=("arbitrary",),
                                             vmem_limit_bytes=VMEM_LIMIT),
        name="in_proj",
    )(x, g1, win16, lng, lnb, ws, bs_full)


def _attn_kernel(rb_ref, q_ref, k_ref, vt_ref, lam_ref, sg_ref, o_ref,
                 diag_scr, corner_scr, qzt_scr, vt1_scr, sa_scr, sb_scr, mxa_scr, mxb_scr, m_scr, acc_scr, *, t):
    b, hg, qi = pl.program_id(0), pl.program_id(1), pl.program_id(2)
    hp = qzt_scr.shape[0]
    c = LANES

    @pl.when((b == 0) & (hg == 0) & (qi == 0))
    def _build_bias_tiles():
        d0 = lax.broadcasted_iota(jnp.int32, (t, t), 1) - lax.broadcasted_iota(jnp.int32, (t, t), 0)
        dc = c + lax.broadcasted_iota(jnp.int32, (c, c), 1) - lax.broadcasted_iota(jnp.int32, (c, c), 0)
        for hh in range(H_B):
            near = _bias_from_distance(jnp.maximum(d0, 0), lambda bk: rb_ref[bk, hh]) * LOG2E
            diag_scr[hh] = jnp.where(d0 >= 0, near, NEG)
            corner_scr[hh] = (_bias_from_distance(dc, lambda bk: rb_ref[bk, hh])
                              - rb_ref[NUM_BUCKETS - 1, hh]) * LOG2E

    depth = lax.broadcasted_iota(jnp.int32, (LANES, t), 0)
    for e in range(hp):
        qt = q_ref[0, :, e * LANES:(e + 1) * LANES].astype(F32).T
        qzt_scr[e] = jnp.concatenate([jnp.where(depth < DH, qt, 0.0), jnp.where(depth >= DH, qt, 0.0)],
                                     axis=1).astype(BF16)
        m_scr[e] = jnp.full(m_scr.shape[1:], -jnp.inf, F32)
        acc_scr[e] = jnp.zeros(acc_scr.shape[1:], F32)

    @pl.when(qi == 0)
    def _stage_values():
        for e in range(hp):
            vt1_scr[e, 0:DV, :] = vt_ref[0, e * DV:(e + 1) * DV, :]
            vt1_scr[e, DV:DV + ONES, :] = jnp.ones((ONES, vt1_scr.shape[2]), BF16)

    def issue(buf, kj):
        s_scr, mx_scr = buf
        for e in range(hp):
            k = k_ref[0, pl.ds(pl.multiple_of(kj * t, t), t), e * LANES:(e + 1) * LANES]
            s = _dot(k, qzt_scr[e])
            s_scr[e] = s
            mx_scr[e] = jnp.max(s, axis=0, keepdims=True)

    def accumulate(e, s, mx, shift, kj):
        m_old = m_scr[e]
        m_new = jnp.maximum(m_old, mx + shift)
        alpha = jnp.exp2(m_old - m_new)
        p = jnp.exp2((s - (m_new - shift)).astype(BF16))
        vt1 = vt1_scr[e, :, pl.ds(pl.multiple_of(kj * t, t), t)]
        acc_scr[e] = alpha * acc_scr[e] + _dot(vt1, p)
        m_scr[e] = m_new

    def consume(buf, kj):
        s_scr, mx_scr = buf
        near = (kj == qi - 1).astype(F32)
        for e in range(hp):
            hh = hg * hp + e
            fix = near * corner_scr[hh]
            c0 = s_scr[e, t - c:t, 0:c] + fix
            c1 = s_scr[e, t - c:t, t:t + c] + fix
            s_scr[e, t - c:t, 0:c] = c0
            s_scr[e, t - c:t, t:t + c] = c1
            mx = mx_scr[e]
            mx = jnp.concatenate([jnp.maximum(mx[:, 0:c], jnp.max(c0, axis=0, keepdims=True)), mx[:, c:t],
                                  jnp.maximum(mx[:, t:t + c], jnp.max(c1, axis=0, keepdims=True)),
                                  mx[:, t + c:]], axis=1)
            accumulate(e, s_scr[e], mx, rb_ref[NUM_BUCKETS - 1, hh] * LOG2E, kj)

    buf_a, buf_b = (sa_scr, mxa_scr), (sb_scr, mxb_scr)
    issue(buf_a, 0)

    def block_pair(i, carry):
        issue(buf_b, 2 * i + 1)
        consume(buf_a, 2 * i)
        issue(buf_a, 2 * i + 2)
        consume(buf_b, 2 * i + 1)
        return carry

    lax.fori_loop(0, qi // 2, block_pair, 0)

    @pl.when(qi % 2 == 1)
    def _odd_block():
        issue(buf_b, qi)
        consume(buf_a, qi - 1)
        sa_scr[...] = sb_scr[...]

    for e in range(hp):
        bias = diag_scr[hg * hp + e]
        s = sa_scr[e]
        s = jnp.concatenate([s[:, 0:t] + bias, s[:, t:2 * t] + bias], axis=1)
        accumulate(e, s, jnp.max(s, axis=0, keepdims=True), 0.0, qi)

    lam = _lambda_full(lam_ref)
    for e in range(hp):
        ot = acc_scr[e, 0:DV, :] / acc_scr[e, DV:DV + 1, :]
        o = (ot[:, 0:t] - lam * ot[:, t:2 * t]).T
        o_ref[0, :, e * DV:(e + 1) * DV] = (_rms(o, sg_ref[...]) * (1.0 - LAMBDA_INIT)).astype(o_ref.dtype)


def _prompt_attention(rel_bias, q, k, vt, lam, sg, *, t, hp):
    b, s, _ = q.shape
    const = lambda bi, hi, qi: (0, 0)
    w = hp * LANES
    return pl.pallas_call(
        functools.partial(_attn_kernel, t=t),
        grid=(b, H_B // hp, s // t),
        in_specs=[pl.BlockSpec(memory_space=pltpu.SMEM),
                  pl.BlockSpec((1, t, w), lambda bi, hi, qi: (bi, qi, hi)),
                  pl.BlockSpec((1, s, w), lambda bi, hi, qi: (bi, 0, hi)),
                  pl.BlockSpec((1, w, s), lambda bi, hi, qi: (bi, hi, 0)),
                  pl.BlockSpec((4, DH), const), pl.BlockSpec((1, DV), const)],
        out_specs=pl.BlockSpec((1, t, w), lambda bi, hi, qi: (bi, qi, hi)),
        out_shape=jax.ShapeDtypeStruct((b, s, D_B), BF16),
        scratch_shapes=[pltpu.VMEM((H_B, t, t), F32), pltpu.VMEM((H_B, LANES, LANES), F32),
                        pltpu.VMEM((hp, LANES, 2 * t), BF16), pltpu.VMEM((hp, DV + ONES, s), BF16),
                        pltpu.VMEM((hp, t, 2 * t), F32), pltpu.VMEM((hp, t, 2 * t), F32),
                        pltpu.VMEM((hp, 1, 2 * t), F32), pltpu.VMEM((hp, 1, 2 * t), F32),
                        pltpu.VMEM((hp, 1, 2 * t), F32), pltpu.VMEM((hp, DV + ONES, 2 * t), F32)],
        compiler_params=pltpu.CompilerParams(dimension_semantics=("arbitrary",) * 3,
                                             vmem_limit_bytes=VMEM_LIMIT),
        name="prompt_attn",
    )(rel_bias, q, k, vt, lam, sg)


def _paged_kernel(pt_ref, rb_ref, q_ref, kn_ref, vn_ref, lam_ref, sg_ref, *rest, pages, dec):
    k_refs, v_refs = rest[:pages], rest[pages:2 * pages]
    o_ref, qh_scr, new_scr, tile_scr, far_scr, m_scr, l_scr, acc_scr = rest[2 * pages:]
    del pt_ref
    b, st = pl.program_id(0), pl.program_id(1)
    n_st = pl.num_programs(1)
    hrows = 2 * dec
    rows = H_B * hrows

    @pl.when((b == 0) & (st == 0))
    def _build_bias_tiles():
        r = lax.broadcasted_iota(jnp.int32, (rows, PAGE), 0)
        j = lax.broadcasted_iota(jnp.int32, (rows, PAGE), 1)
        head, i = r // hrows, r % dec

        def row_value(bk):
            val = jnp.zeros((rows, PAGE), F32) + rb_ref[bk, H_B - 1]
            for hh in range(H_B - 2, -1, -1):
                val = jnp.where(head == hh, rb_ref[bk, hh], val)
            return val

        tile_scr[0] = _bias_from_distance(PAGE + i - j, row_value)
        tile_scr[1] = jnp.where(j <= i, _bias_from_distance(jnp.maximum(i - j, 0), row_value), NEG)
        far_scr[...] = row_value(NUM_BUCKETS - 1)[:, 0:1]
        new_scr[...] = jnp.zeros_like(new_scr)

    @pl.when(st == 0)
    def _start_batch():
        r = lax.broadcasted_iota(jnp.int32, (hrows, LANES), 0)
        lane = lax.broadcasted_iota(jnp.int32, (hrows, LANES), 1)
        for hh in range(H_B):
            qh = q_ref[0, :, hh * LANES:(hh + 1) * LANES].astype(F32)
            qh = jnp.concatenate([qh, qh], axis=0)
            qh_scr[hh] = jnp.where(lane // DH == r // dec, qh, 0.0).astype(BF16)
        m_scr[...] = jnp.full_like(m_scr, -jnp.inf)
        l_scr[...] = jnp.zeros_like(l_scr)
        acc_scr[...] = jnp.zeros_like(acc_scr)

    def scores(head_keys):
        return jnp.concatenate([_dot_nt(qh_scr[hh], head_keys(hh)) for hh in range(H_B)], axis=0)

    def update(s, shift, head_values):
        m_old = m_scr[...]
        m_new = jnp.maximum(m_old, jnp.max(s, axis=-1, keepdims=True) + shift)
        alpha = jnp.exp(m_old - m_new)
        p = jnp.exp(s - (m_new - shift))
        l_scr[...] = alpha * l_scr[...] + jnp.sum(p, axis=-1, keepdims=True)
        p = p.astype(BF16)
        pv = jnp.concatenate([_dot(p[hh * hrows:(hh + 1) * hrows], head_values(hh))
                              for hh in range(H_B)], axis=0)
        acc_scr[...] = alpha * acc_scr[...] + pv
        m_scr[...] = m_new

    def cached(refs, first, last):
        return lambda hh: jnp.concatenate(
            [refs[p][0, pl.ds(hh, PAGE, stride=H_B), :] for p in range(first, last)], axis=0).astype(BF16)

    def fresh(which):
        return lambda hh: new_scr[which, :, hh * LANES:(hh + 1) * LANES].astype(BF16)

    @pl.when(st < n_st - 1)
    def _far_pages():
        update(scores(cached(k_refs, 0, pages)), far_scr[...], cached(v_refs, 0, pages))

    @pl.when(st == n_st - 1)
    def _last_pages_and_new_rows():
        if pages > 1:
            update(scores(cached(k_refs, 0, pages - 1)), far_scr[...], cached(v_refs, 0, pages - 1))
        update(scores(cached(k_refs, pages - 1, pages)) + tile_scr[0], 0.0,
               cached(v_refs, pages - 1, pages))
        new_scr[0, 0:dec, :] = kn_ref[0]
        new_scr[1, 0:dec, :] = vn_ref[0]
        update(scores(fresh(0)) + tile_scr[1], 0.0, fresh(1))

        o = acc_scr[...] / l_scr[...]
        lam = _lambda_full(lam_ref)
        for hh in range(H_B):
            r0 = hh * hrows
            oh = o[r0:r0 + dec, :] - lam * o[r0 + dec:r0 + hrows, :]
            o_ref[0, :, hh * DV:(hh + 1) * DV] = (_rms(oh, sg_ref[...])
                                                  * (1.0 - LAMBDA_INIT)).astype(o_ref.dtype)


def _sample_attention(page_table, rel_bias, q, k_new, v_new, cache_k, cache_v, lam, sg, *, pages):
    nb, dec, _ = q.shape
    n_pages = page_table.shape[1]
    n_st = n_pages // pages
    rows = 2 * H_B * dec
    const2 = lambda bi, si, pt: (0, 0)
    per_batch = pl.BlockSpec((1, dec, D_QK), lambda bi, si, pt: (bi, 0, 0))

    def page_spec(p):
        return pl.BlockSpec((1, PAGE * H_B, LANES),
                            lambda bi, si, pt: (pt[bi * n_pages + si * pages + p], 0, 0))

    grid_spec = pltpu.PrefetchScalarGridSpec(
        num_scalar_prefetch=1,
        grid=(nb, n_st),
        in_specs=[pl.BlockSpec(memory_space=pltpu.SMEM), per_batch, per_batch, per_batch,
                  pl.BlockSpec((4, DH), const2), pl.BlockSpec((1, DV), const2)]
                 + [page_spec(p) for p in range(pages)] * 2,
        out_specs=per_batch,
        scratch_shapes=[pltpu.VMEM((H_B, 2 * dec, LANES), BF16), pltpu.VMEM((2, PAGE, D_QK), F32),
                        pltpu.VMEM((2, rows, PAGE), F32), pltpu.VMEM((rows, 1), F32),
                        pltpu.VMEM((rows, 1), F32), pltpu.VMEM((rows, 1), F32),
                        pltpu.VMEM((rows, DV), F32)],
    )
    return pl.pallas_call(
        functools.partial(_paged_kernel, pages=pages, dec=dec),
        grid_spec=grid_spec,
        out_shape=jax.ShapeDtypeStruct((nb, dec, D_B), BF16),
        compiler_params=pltpu.CompilerParams(dimension_semantics=("arbitrary", "arbitrary"),
                                             vmem_limit_bytes=VMEM_LIMIT),
        name="paged_attn",
    )(page_table.reshape(-1), rel_bias, q, k_new, v_new, lam, sg,
      *([cache_k] * pages), *([cache_v] * pages))


def _tail_kernel(x_ref, a_ref, b_ref, wo_ref, g2_ref, wg_ref, wu_ref, wd_ref, gf_ref, y_ref):
    h = x_ref[...] + _dot(a_ref[...], wo_ref[0:D_A, :]) + _dot(b_ref[...], wo_ref[D_A:, :])
    hn = _rms(h, g2_ref[...]).astype(BF16)
    gate = _dot(hn, wg_ref[...])
    f = (gate * (1.0 / (1.0 + jnp.exp(-gate))) * _dot(hn, wu_ref[...])).astype(BF16)
    y_ref[...] = _rms(h + _dot(f, wd_ref[...]), gf_ref[...])


def _tail(x, a, bmix, wo16, g2, wg16, wu16, wd16, gf, *, tm):
    t, d = x.shape
    d_ff = wg16.shape[1]
    const = lambda i: (0, 0)
    resident = lambda shape: pl.BlockSpec(shape, const, pipeline_mode=pl.Buffered(1))
    return pl.pallas_call(
        _tail_kernel,
        grid=(t // tm,),
        in_specs=[pl.BlockSpec((tm, d), lambda i: (i, 0)), pl.BlockSpec((tm, D_A), lambda i: (i, 0)),
                  pl.BlockSpec((tm, D_B), lambda i: (i, 0)), resident((D_A + D_B, d)),
                  pl.BlockSpec((1, d), const), resident((d, d_ff)), resident((d, d_ff)),
                  resident((d_ff, d)), pl.BlockSpec((1, d), const)],
        out_specs=pl.BlockSpec((tm, d), lambda i: (i, 0)),
        out_shape=jax.ShapeDtypeStruct((t, d), F32),
        compiler_params=pltpu.CompilerParams(dimension_semantics=("arbitrary",),
                                             vmem_limit_bytes=VMEM_LIMIT),
        name="tail",
    )(x, a, bmix, wo16, g2, wg16, wu16, wd16, gf)


def kernel(x_prompt, x_sample, cache_k, cache_v, page_table, norm1_g, w_in, gmlp_ln_g, gmlp_ln_b, gmlp_ws, gmlp_bs, lambda_q1, lambda_k1, lambda_q2, lambda_k2, subln_g, rel_bias, w_out, norm2_g, w_gate, w_up, w_down, final_g):
    nb, seq, d = x_prompt.shape
    db, dec, _ = x_sample.shape
    row = lambda p: p.reshape(1, -1)

    win16, wo16 = w_in.astype(BF16), w_out.astype(BF16)
    wg16, wu16, wd16 = w_gate.astype(BF16), w_up.astype(BF16), w_down.astype(BF16)
    lam = jnp.stack([lambda_q1, lambda_k1, lambda_q2, lambda_k2])
    proj_params = (row(norm1_g), win16, row(gmlp_ln_g), row(gmlp_ln_b))
    tail_params = (wo16, row(norm2_g), wg16, wu16, wd16, row(final_g))

    xp = x_prompt.reshape(nb * seq, d)
    bs_p = jnp.repeat(gmlp_bs.T, CH_A, axis=1)
    a_p, q_p, kf_p, vf_p, kb_p, vt_p = _in_proj(xp, *proj_params, gmlp_ws, bs_p,
                                               tm=512, cs=CHUNK, seq=seq)
    shape3 = lambda z: z.reshape(nb, seq, -1)
    b_p = _prompt_attention(rel_bias, shape3(q_p), shape3(kb_p), vt_p, lam, row(subln_g), t=512, hp=2)
    y_p = _tail(xp, a_p, b_p.reshape(nb * seq, D_B), *tail_params, tm=512)

    xs = x_sample.reshape(db * dec, d)
    ws_s = jnp.einsum('ab,gts->gatbs', jnp.eye(db, dtype=F32),
                      gmlp_ws[:, :dec, :dec]).reshape(G_A, db * dec, db * dec)
    bs_s = jnp.tile(jnp.repeat(gmlp_bs[:, :dec].T, CH_A, axis=1), (db, 1))
    a_s, q_s, kf_s, vf_s, gv_s = _in_proj(xs, *proj_params, ws_s, bs_s, tm=db * dec, cs=db * dec)
    b_s = _sample_attention(page_table, rel_bias, q_s.reshape(db, dec, D_QK),
                            kf_s.reshape(db, dec, D_QK), vf_s.reshape(db, dec, D_B),
                            cache_k.reshape(-1, PAGE * H_B, DV), cache_v.reshape(-1, PAGE * H_B, DV),
                            lam, row(subln_g), pages=16)
    y_s = _tail(xs, a_s, b_s.reshape(db * dec, D_B), *tail_params, tm=db * dec)

    return (y_p.reshape(nb, seq, d), y_s.reshape(db, dec, d),
            kf_p.reshape(nb, seq, H_B, 2 * DH), vf_p.reshape(nb, seq, H_B, DV),
            kf_s.reshape(db, dec, H_B, 2 * DH), vf_s.reshape(db, dec, H_B, DV),
            gv_s.reshape(db, dec, D_A))
```

```python
import functools
import math

import numpy as np
import jax
import jax.numpy as jnp
from jax import lax
from jax.experimental import pallas as pl
from jax.experimental.pallas import tpu as pltpu

F32 = jnp.float32
BF16 = jnp.bfloat16

EPS = 1e-6
LANES = 128
G_A = 4
CH_A = 128
CHUNK = 128
H_B = 4
DV = 128
DH = 64
D_A = G_A * CH_A
D_QK = H_B * 2 * DH
D_B = H_B * DV
SCALE = DH ** -0.5
LOG2E = math.log2(math.e)
LAMBDA_INIT = 0.8 - 0.6 * math.exp(-0.3 * 0)
NUM_BUCKETS = 32
MAX_EXACT = NUM_BUCKETS // 2
MAX_DISTANCE = 128
PAGE = 128
ONES = 16
NEG = -1e30
VMEM_LIMIT = 56 * 1024 * 1024


def _bucket_thresholds():
    n = np.arange(0, 2 * MAX_DISTANCE)
    nf = np.maximum(n, 1).astype(np.float64)
    large = MAX_EXACT + (np.log(nf / MAX_EXACT) / math.log(MAX_DISTANCE / MAX_EXACT)
                         * (NUM_BUCKETS - MAX_EXACT)).astype(np.int32)
    bucket = np.where(n < MAX_EXACT, n, np.minimum(large, NUM_BUCKETS - 1))
    assert np.all(np.diff(bucket) >= 0) and bucket[MAX_DISTANCE - 1] == NUM_BUCKETS - 1
    return [int(n[bucket >= b].min()) for b in range(NUM_BUCKETS)]


BUCKET_START = _bucket_thresholds()


def _rms(x, g):
    ms = jnp.mean(x * x, axis=-1, keepdims=True)
    return x * lax.rsqrt(ms + EPS) * g


def _dot(a, b):
    return jnp.dot(a, b, preferred_element_type=F32)


def _dot_nt(a, b):
    return lax.dot_general(a, b, (((1,), (1,)), ((), ())), preferred_element_type=F32)


def _bias_from_distance(n, value_of_bucket):
    val = jnp.zeros(n.shape, F32) + value_of_bucket(0)
    for b in range(1, NUM_BUCKETS):
        val = jnp.where(n >= BUCKET_START[b], value_of_bucket(b), val)
    return val


def _lambda_full(lam_ref):
    s1 = jnp.sum(lam_ref[0:1, :] * lam_ref[1:2, :], axis=-1, keepdims=True)
    s2 = jnp.sum(lam_ref[2:3, :] * lam_ref[3:4, :], axis=-1, keepdims=True)
    return jnp.exp(s1) - jnp.exp(s2) + LAMBDA_INIT


def _inproj_kernel(x_ref, g1_ref, win_ref, lng_ref, lnb_ref, ws_ref, bs_ref, *out_refs, cs, prompt):
    if prompt:
        a_ref, q_ref, kf_ref, vf_ref, kb_ref, vt_ref = out_refs
    else:
        a_ref, q_ref, kf_ref, vf_ref, vn_ref = out_refs
    tm = x_ref.shape[0]
    xn = _rms(x_ref[...], g1_ref[...]).astype(BF16)

    q = _dot(xn, win_ref[:, 2 * D_A:2 * D_A + D_QK])
    if prompt:
        q_ref[0] = (q * (SCALE * LOG2E)).T.astype(BF16)
    else:
        q_ref[...] = (q * SCALE).astype(BF16)
    k = _dot(xn, win_ref[:, 2 * D_A + D_QK:2 * D_A + 2 * D_QK])
    vb = _dot(xn, win_ref[:, 2 * D_A + 2 * D_QK:])
    for hh in range(H_B):
        kf_ref[pl.ds(hh, tm, stride=H_B), :] = k[:, hh * LANES:(hh + 1) * LANES]
        vf_ref[pl.ds(hh, tm, stride=H_B), :] = vb[:, hh * LANES:(hh + 1) * LANES]
    if prompt:
        kb_ref[...] = k.astype(BF16)
        vt_ref[0] = vb.T.astype(BF16)

    u = _dot(xn, win_ref[:, 0:D_A])
    v = _dot(xn, win_ref[:, D_A:2 * D_A])
    row = lax.broadcasted_iota(jnp.int32, (cs, cs), 0)
    col = lax.broadcasted_iota(jnp.int32, (cs, cs), 1)
    for g in range(G_A):
        gs = slice(g * CH_A, (g + 1) * CH_A)
        vg = v[:, gs]
        mu = jnp.mean(vg, axis=-1, keepdims=True)
        d = vg - mu
        var = jnp.mean(d * d, axis=-1, keepdims=True)
        vn = d * lax.rsqrt(var + EPS) * lng_ref[:, gs] + lnb_ref[:, gs]
        if not prompt:
            vn_ref[:, gs] = vn
        vn16 = vn.astype(BF16)
        w = jnp.where(row >= col, ws_ref[g], 0.0).astype(BF16)
        for c in range(tm // cs):
            rs = slice(c * cs, (c + 1) * cs)
            s = _dot(w, vn16[rs, :]) + bs_ref[:, gs]
            a_ref[rs, gs] = (u[rs, gs] * s).astype(BF16)


def _in_proj(x, g1, win16, lng, lnb, ws, bs_full, *, tm, cs, seq=None):
    t, d = x.shape
    d_in = win16.shape[1]
    const = lambda i: (0, 0)
    tile = lambda w: pl.BlockSpec((tm, w), lambda i: (i, 0))
    heads = pl.BlockSpec((tm * H_B, LANES), lambda i: (i, 0))
    if seq is not None:
        tiles = seq // tm
        transposed = pl.BlockSpec((1, D_QK, tm), lambda i: (i // tiles, 0, i % tiles))
        transposed_shape = jax.ShapeDtypeStruct((t // seq, D_QK, seq), BF16)
    out_shape = [jax.ShapeDtypeStruct((t, D_A), BF16),
                 jax.ShapeDtypeStruct((t, D_QK), BF16) if seq is None else transposed_shape,
                 jax.ShapeDtypeStruct((t * H_B, LANES), F32), jax.ShapeDtypeStruct((t * H_B, LANES), F32)]
    out_specs = [tile(D_A), tile(D_QK) if seq is None else transposed, heads, heads]
    if seq is not None:
        out_shape += [jax.ShapeDtypeStruct((t, D_QK), BF16), transposed_shape]
        out_specs += [tile(D_QK), transposed]
    else:
        out_shape.append(jax.ShapeDtypeStruct((t, D_A), F32))
        out_specs.append(tile(D_A))
    return pl.pallas_call(
        functools.partial(_inproj_kernel, cs=cs, prompt=seq is not None),
        grid=(t // tm,),
        in_specs=[tile(d), pl.BlockSpec((1, d), const), pl.BlockSpec((d, d_in), const),
                  pl.BlockSpec((1, D_A), const), pl.BlockSpec((1, D_A), const),
                  pl.BlockSpec((G_A, cs, cs), lambda i: (0, 0, 0)), pl.BlockSpec((cs, D_A), const)],
        out_specs=out_specs,
        out_shape=out_shape,
        compiler_params=pltpu.CompilerParams(dimension_semantics=("arbitrary",),
                                             vmem_limit_bytes=VMEM_LIMIT),
        name="in_proj",
    )(x, g1, win16, lng, lnb, ws, bs_full)


def _attn_kernel(rb_ref, q_ref, k_ref, vt_ref, lam_ref, sg_ref, o_ref,
                 diag_scr, corner_scr, qzt_scr, vt1_scr, sa_scr, sb_scr, mxa_scr, mxb_scr, m_scr, acc_scr, *, t):
    b, hg, qi = pl.program_id(0), pl.program_id(1), pl.program_id(2)
    hp = qzt_scr.shape[0]
    c = LANES

    @pl.when((b == 0) & (hg == 0) & (qi == 0))
    def _build_bias_tiles():
        d0 = lax.broadcasted_iota(jnp.int32, (t, t), 1) - lax.broadcasted_iota(jnp.int32, (t, t), 0)
        dc = c + lax.broadcasted_iota(jnp.int32, (c, c), 1) - lax.broadcasted_iota(jnp.int32, (c, c), 0)
        for hh in range(H_B):
            near = _bias_from_distance(jnp.maximum(d0, 0), lambda bk: rb_ref[bk, hh]) * LOG2E
            diag_scr[hh] = jnp.where(d0 >= 0, near, NEG)
            corner_scr[hh] = (_bias_from_distance(dc, lambda bk: rb_ref[bk, hh])
                              - rb_ref[NUM_BUCKETS - 1, hh]) * LOG2E

    depth = lax.broadcasted_iota(jnp.int32, (LANES, t), 0)
    for e in range(hp):
        qt = q_ref[0, e * LANES:(e + 1) * LANES, :]
        zero = jnp.zeros_like(qt)
        qzt_scr[e] = jnp.concatenate([jnp.where(depth < DH, qt, zero), jnp.where(depth >= DH, qt, zero)], axis=1)
        m_scr[e] = jnp.full(m_scr.shape[1:], -jnp.inf, F32)
        acc_scr[e] = jnp.zeros(acc_scr.shape[1:], F32)

    @pl.when(qi == 0)
    def _stage_values():
        for e in range(hp):
            vt1_scr[e, 0:DV, :] = vt_ref[0, e * DV:(e + 1) * DV, :]
            vt1_scr[e, DV:DV + ONES, :] = jnp.ones((ONES, vt1_scr.shape[2]), BF16)

    def issue(buf, kj):
        s_scr, mx_scr = buf
        for e in range(hp):
            k = k_ref[0, pl.ds(pl.multiple_of(kj * t, t), t), e * LANES:(e + 1) * LANES]
            s = _dot(k, qzt_scr[e])
            s_scr[e] = s
            mx_scr[e] = jnp.max(s, axis=0, keepdims=True)

    def accumulate(e, s, mx, shift, kj):
        m_old = m_scr[e]
        m_new = jnp.maximum(m_old, mx + shift)
        alpha = jnp.exp2(m_old - m_new)
        p = jnp.exp2((s - (m_new - shift)).astype(BF16))
        vt1 = vt1_scr[e, :, pl.ds(pl.multiple_of(kj * t, t), t)]
        acc_scr[e] = alpha * acc_scr[e] + _dot(vt1, p)
        m_scr[e] = m_new

    def consume(buf, kj):
        s_scr, mx_scr = buf
        near = (kj == qi - 1).astype(F32)
        for e in range(hp):
            hh = hg * hp + e
            fix = near * corner_scr[hh]
            c0 = s_scr[e, t - c:t, 0:c] + fix
            c1 = s_scr[e, t - c:t, t:t + c] + fix
            s_scr[e, t - c:t, 0:c] = c0
            s_scr[e, t - c:t, t:t + c] = c1
            mx = mx_scr[e]
            mx = jnp.concatenate([jnp.maximum(mx[:, 0:c], jnp.max(c0, axis=0, keepdims=True)), mx[:, c:t],
                                  jnp.maximum(mx[:, t:t + c], jnp.max(c1, axis=0, keepdims=True)),
                                  mx[:, t + c:]], axis=1)
            accumulate(e, s_scr[e], mx, rb_ref[NUM_BUCKETS - 1, hh] * LOG2E, kj)

    buf_a, buf_b = (sa_scr, mxa_scr), (sb_scr, mxb_scr)
    issue(buf_a, 0)

    def block_pair(i, carry):
        issue(buf_b, 2 * i + 1)
        consume(buf_a, 2 * i)
        issue(buf_a, 2 * i + 2)
        consume(buf_b, 2 * i + 1)
        return carry

    lax.fori_loop(0, qi // 2, block_pair, 0)

    @pl.when(qi % 2 == 1)
    def _odd_block():
        issue(buf_b, qi)
        consume(buf_a, qi - 1)
        sa_scr[...] = sb_scr[...]

    for e in range(hp):
        bias = diag_scr[hg * hp + e]
        s = sa_scr[e]
        s = jnp.concatenate([s[:, 0:t] + bias, s[:, t:2 * t] + bias], axis=1)
        accumulate(e, s, jnp.max(s, axis=0, keepdims=True), 0.0, qi)

    lam = _lambda_full(lam_ref)
    for e in range(hp):
        ot = acc_scr[e, 0:DV, :] / acc_scr[e, DV:DV + 1, :]
        ot = ot[:, 0:t] - lam * ot[:, t:2 * t]
        ms = jnp.mean(ot * ot, axis=0, keepdims=True)
        o_ref[0, e * DV:(e + 1) * DV, :] = (ot * lax.rsqrt(ms + EPS) * sg_ref[...]
                                            * (1.0 - LAMBDA_INIT)).astype(o_ref.dtype)


def _prompt_attention(rel_bias, qt, k, vt, lam, sg_col, *, t, hp):
    b, s, _ = k.shape
    const = lambda bi, hi, qi: (0, 0)
    w = hp * LANES
    return pl.pallas_call(
        functools.partial(_attn_kernel, t=t),
        grid=(b, H_B // hp, s // t),
        in_specs=[pl.BlockSpec(memory_space=pltpu.SMEM),
                  pl.BlockSpec((1, w, t), lambda bi, hi, qi: (bi, hi, qi)),
                  pl.BlockSpec((1, s, w), lambda bi, hi, qi: (bi, 0, hi)),
                  pl.BlockSpec((1, w, s), lambda bi, hi, qi: (bi, hi, 0)),
                  pl.BlockSpec((4, DH), const), pl.BlockSpec((DV, 1), const)],
        out_specs=pl.BlockSpec((1, w, t), lambda bi, hi, qi: (bi, hi, qi)),
        out_shape=jax.ShapeDtypeStruct((b, D_B, s), BF16),
        scratch_shapes=[pltpu.VMEM((H_B, t, t), F32), pltpu.VMEM((H_B, LANES, LANES), F32),
                        pltpu.VMEM((hp, LANES, 2 * t), BF16), pltpu.VMEM((hp, DV + ONES, s), BF16),
                        pltpu.VMEM((hp, t, 2 * t), F32), pltpu.VMEM((hp, t, 2 * t), F32),
                        pltpu.VMEM((hp, 1, 2 * t), F32), pltpu.VMEM((hp, 1, 2 * t), F32),
                        pltpu.VMEM((hp, 1, 2 * t), F32), pltpu.VMEM((hp, DV + ONES, 2 * t), F32)],
        compiler_params=pltpu.CompilerParams(dimension_semantics=("arbitrary",) * 3,
                                             vmem_limit_bytes=VMEM_LIMIT),
        name="prompt_attn",
    )(rel_bias, qt, k, vt, lam, sg_col)


def _paged_kernel(pt_ref, rb_ref, q_ref, kn_ref, vn_ref, lam_ref, sg_ref, *rest, pages, dec):
    k_refs, v_refs = rest[:pages], rest[pages:2 * pages]
    o_ref, qh_scr, new_scr, tile_scr, far_scr, m_scr, l_scr, acc_scr = rest[2 * pages:]
    del pt_ref
    b, st = pl.program_id(0), pl.program_id(1)
    n_st = pl.num_programs(1)
    hrows = 2 * dec
    rows = H_B * hrows

    @pl.when((b == 0) & (st == 0))
    def _build_bias_tiles():
        r = lax.broadcasted_iota(jnp.int32, (rows, PAGE), 0)
        j = lax.broadcasted_iota(jnp.int32, (rows, PAGE), 1)
        head, i = r // hrows, r % dec

        def row_value(bk):
            val = jnp.zeros((rows, PAGE), F32) + rb_ref[bk, H_B - 1]
            for hh in range(H_B - 2, -1, -1):
                val = jnp.where(head == hh, rb_ref[bk, hh], val)
            return val

        tile_scr[0] = _bias_from_distance(PAGE + i - j, row_value)
        tile_scr[1] = jnp.where(j <= i, _bias_from_distance(jnp.maximum(i - j, 0), row_value), NEG)
        far_scr[...] = row_value(NUM_BUCKETS - 1)[:, 0:1]
        new_scr[...] = jnp.zeros_like(new_scr)

    @pl.when(st == 0)
    def _start_batch():
        r = lax.broadcasted_iota(jnp.int32, (hrows, LANES), 0)
        lane = lax.broadcasted_iota(jnp.int32, (hrows, LANES), 1)
        for hh in range(H_B):
            qh = q_ref[0, :, hh * LANES:(hh + 1) * LANES].astype(F32)
            qh = jnp.concatenate([qh, qh], axis=0)
            qh_scr[hh] = jnp.where(lane // DH == r // dec, qh, 0.0).astype(BF16)
        m_scr[...] = jnp.full_like(m_scr, -jnp.inf)
        l_scr[...] = jnp.zeros_like(l_scr)
        acc_scr[...] = jnp.zeros_like(acc_scr)

    def scores(head_keys):
        return jnp.concatenate([_dot_nt(qh_scr[hh], head_keys(hh)) for hh in range(H_B)], axis=0)

    def update(s, shift, head_values):
        m_old = m_scr[...]
        m_new = jnp.maximum(m_old, jnp.max(s, axis=-1, keepdims=True) + shift)
        alpha = jnp.exp(m_old - m_new)
        p = jnp.exp(s - (m_new - shift))
        l_scr[...] = alpha * l_scr[...] + jnp.sum(p, axis=-1, keepdims=True)
        p = p.astype(BF16)
        pv = jnp.concatenate([_dot(p[hh * hrows:(hh + 1) * hrows], head_values(hh))
                              for hh in range(H_B)], axis=0)
        acc_scr[...] = alpha * acc_scr[...] + pv
        m_scr[...] = m_new

    def cached(refs, first, last):
        return lambda hh: jnp.concatenate(
            [refs[p][0, pl.ds(hh, PAGE, stride=H_B), :] for p in range(first, last)], axis=0).astype(BF16)

    def fresh(which):
        return lambda hh: new_scr[which, :, hh * LANES:(hh + 1) * LANES].astype(BF16)

    @pl.when(st < n_st - 1)
    def _far_pages():
        update(scores(cached(k_refs, 0, pages)), far_scr[...], cached(v_refs, 0, pages))

    @pl.when(st == n_st - 1)
    def _last_pages_and_new_rows():
        if pages > 1:
            update(scores(cached(k_refs, 0, pages - 1)), far_scr[...], cached(v_refs, 0, pages - 1))
        update(scores(cached(k_refs, pages - 1, pages)) + tile_scr[0], 0.0,
               cached(v_refs, pages - 1, pages))
        new_scr[0, 0:dec, :] = kn_ref[0]
        new_scr[1, 0:dec, :] = vn_ref[0]
        update(scores(fresh(0)) + tile_scr[1], 0.0, fresh(1))

        o = acc_scr[...] / l_scr[...]
        lam = _lambda_full(lam_ref)
        for hh in range(H_B):
            r0 = hh * hrows
            oh = o[r0:r0 + dec, :] - lam * o[r0 + dec:r0 + hrows, :]
            o_ref[0, :, hh * DV:(hh + 1) * DV] = (_rms(oh, sg_ref[...])
                                                  * (1.0 - LAMBDA_INIT)).astype(o_ref.dtype)


def _sample_attention(page_table, rel_bias, q, k_new, v_new, cache_k, cache_v, lam, sg, *, pages):
    nb, dec, _ = q.shape
    n_pages = page_table.shape[1]
    n_st = n_pages // pages
    rows = 2 * H_B * dec
    const2 = lambda bi, si, pt: (0, 0)
    per_batch = pl.BlockSpec((1, dec, D_QK), lambda bi, si, pt: (bi, 0, 0))

    def page_spec(p):
        return pl.BlockSpec((1, PAGE * H_B, LANES),
                            lambda bi, si, pt: (pt[bi * n_pages + si * pages + p], 0, 0))

    grid_spec = pltpu.PrefetchScalarGridSpec(
        num_scalar_prefetch=1,
        grid=(nb, n_st),
        in_specs=[pl.BlockSpec(memory_space=pltpu.SMEM), per_batch, per_batch, per_batch,
                  pl.BlockSpec((4, DH), const2), pl.BlockSpec((1, DV), const2)]
                 + [page_spec(p) for p in range(pages)] * 2,
        out_specs=per_batch,
        scratch_shapes=[pltpu.VMEM((H_B, 2 * dec, LANES), BF16), pltpu.VMEM((2, PAGE, D_QK), F32),
                        pltpu.VMEM((2, rows, PAGE), F32), pltpu.VMEM((rows, 1), F32),
                        pltpu.VMEM((rows, 1), F32), pltpu.VMEM((rows, 1), F32),
                        pltpu.VMEM((rows, DV), F32)],
    )
    return pl.pallas_call(
        functools.partial(_paged_kernel, pages=pages, dec=dec),
        grid_spec=grid_spec,
        out_shape=jax.ShapeDtypeStruct((nb, dec, D_B), BF16),
        compiler_params=pltpu.CompilerParams(dimension_semantics=("arbitrary", "arbitrary"),
                                             vmem_limit_bytes=VMEM_LIMIT),
        name="paged_attn",
    )(page_table.reshape(-1), rel_bias, q, k_new, v_new, lam, sg,
      *([cache_k] * pages), *([cache_v] * pages))


def _tail_kernel(x_ref, a_ref, b_ref, wo_ref, g2_ref, wg_ref, wu_ref, wd_ref, gf_ref, y_ref, *, b_transposed):
    if b_transposed:
        bw = lax.dot_general(b_ref[0], wo_ref[D_A:, :], (((0,), (0,)), ((), ())), preferred_element_type=F32)
    else:
        bw = _dot(b_ref[...], wo_ref[D_A:, :])
    h = x_ref[...] + _dot(a_ref[...], wo_ref[0:D_A, :]) + bw
    hn = _rms(h, g2_ref[...]).astype(BF16)
    gate = _dot(hn, wg_ref[...])
    f = (gate * (1.0 / (1.0 + jnp.exp(-gate))) * _dot(hn, wu_ref[...])).astype(BF16)
    y_ref[...] = _rms(h + _dot(f, wd_ref[...]), gf_ref[...])


def _tail(x, a, bmix, wo16, g2, wg16, wu16, wd16, gf, *, tm):
    t, d = x.shape
    d_ff = wg16.shape[1]
    const = lambda i: (0, 0)
    resident = lambda shape: pl.BlockSpec(shape, const, pipeline_mode=pl.Buffered(1))
    b_transposed = bmix.ndim == 3
    if b_transposed:
        tiles = bmix.shape[2] // tm
        b_spec = pl.BlockSpec((1, D_B, tm), lambda i: (i // tiles, 0, i % tiles))
    else:
        b_spec = pl.BlockSpec((tm, D_B), lambda i: (i, 0))
    return pl.pallas_call(
        functools.partial(_tail_kernel, b_transposed=b_transposed),
        grid=(t // tm,),
        in_specs=[pl.BlockSpec((tm, d), lambda i: (i, 0)), pl.BlockSpec((tm, D_A), lambda i: (i, 0)),
                  b_spec, resident((D_A + D_B, d)),
                  pl.BlockSpec((1, d), const), resident((d, d_ff)), resident((d, d_ff)),
                  resident((d_ff, d)), pl.BlockSpec((1, d), const)],
        out_specs=pl.BlockSpec((tm, d), lambda i: (i, 0)),
        out_shape=jax.ShapeDtypeStruct((t, d), F32),
        compiler_params=pltpu.CompilerParams(dimension_semantics=("arbitrary",),
                                             vmem_limit_bytes=VMEM_LIMIT),
        name="tail",
    )(x, a, bmix, wo16, g2, wg16, wu16, wd16, gf)


def kernel(x_prompt, x_sample, cache_k, cache_v, page_table, norm1_g, w_in, gmlp_ln_g, gmlp_ln_b, gmlp_ws, gmlp_bs, lambda_q1, lambda_k1, lambda_q2, lambda_k2, subln_g, rel_bias, w_out, norm2_g, w_gate, w_up, w_down, final_g):
    nb, seq, d = x_prompt.shape
    db, dec, _ = x_sample.shape
    row = lambda p: p.reshape(1, -1)

    win16, wo16 = w_in.astype(BF16), w_out.astype(BF16)
    wg16, wu16, wd16 = w_gate.astype(BF16), w_up.astype(BF16), w_down.astype(BF16)
    lam = jnp.stack([lambda_q1, lambda_k1, lambda_q2, lambda_k2])
    proj_params = (row(norm1_g), win16, row(gmlp_ln_g), row(gmlp_ln_b))
    tail_params = (wo16, row(norm2_g), wg16, wu16, wd16, row(final_g))

    xp = x_prompt.reshape(nb * seq, d)
    bs_p = jnp.repeat(gmlp_bs.T, CH_A, axis=1)
    a_p, qt_p, kf_p, vf_p, kb_p, vt_p = _in_proj(xp, *proj_params, gmlp_ws, bs_p,
                                                tm=512, cs=CHUNK, seq=seq)
    bt_p = _prompt_attention(rel_bias, qt_p, kb_p.reshape(nb, seq, D_QK), vt_p, lam,
                             subln_g.reshape(DV, 1), t=512, hp=2)
    y_p = _tail(xp, a_p, bt_p, *tail_params, tm=512)

    xs = x_sample.reshape(db * dec, d)
    ws_s = jnp.einsum('ab,gts->gatbs', jnp.eye(db, dtype=F32),
                      gmlp_ws[:, :dec, :dec]).reshape(G_A, db * dec, db * dec)
    bs_s = jnp.tile(jnp.repeat(gmlp_bs[:, :dec].T, CH_A, axis=1), (db, 1))
    a_s, q_s, kf_s, vf_s, gv_s = _in_proj(xs, *proj_params, ws_s, bs_s, tm=db * dec, cs=db * dec)
    b_s = _sample_attention(page_table, rel_bias, q_s.reshape(db, dec, D_QK),
                            kf_s.reshape(db, dec, D_QK), vf_s.reshape(db, dec, D_B),
                            cache_k.reshape(-1, PAGE * H_B, DV), cache_v.reshape(-1, PAGE * H_B, DV),
                            lam, row(subln_g), pages=16)
    y_s = _tail(xs, a_s, b_s.reshape(db * dec, D_B), *tail_params, tm=db * dec)

    return (y_p.reshape(nb, seq, d), y_s.reshape(db, dec, d),
            kf_p.reshape(nb, seq, H_B, 2 * DH), vf_p.reshape(nb, seq, H_B, DV),
            kf_s.reshape(db, dec, H_B, 2 * DH), vf_s.reshape(db, dec, H_B, DV),
            gv_s.reshape(db, dec, D_A))
```

```python
import functools
import math

import numpy as np
import jax
import jax.numpy as jnp
from jax import lax
from jax.experimental import pallas as pl
from jax.experimental.pallas import tpu as pltpu

F32 = jnp.float32
BF16 = jnp.bfloat16

EPS = 1e-6
LANES = 128
G_A = 4
CH_A = 128
CHUNK = 128
H_B = 4
DV = 128
DH = 64
D_A = G_A * CH_A
D_QK = H_B * 2 * DH
D_B = H_B * DV
SCALE = DH ** -0.5
LOG2E = math.log2(math.e)
LAMBDA_INIT = 0.8 - 0.6 * math.exp(-0.3 * 0)
NUM_BUCKETS = 32
MAX_EXACT = NUM_BUCKETS // 2
MAX_DISTANCE = 128
PAGE = 128
ONES = 16
NEG = -1e30
VMEM_LIMIT = 56 * 1024 * 1024


def _bucket_thresholds():
    n = np.arange(0, 2 * MAX_DISTANCE)
    nf = np.maximum(n, 1).astype(np.float64)
    large = MAX_EXACT + (np.log(nf / MAX_EXACT) / math.log(MAX_DISTANCE / MAX_EXACT)
                         * (NUM_BUCKETS - MAX_EXACT)).astype(np.int32)
    bucket = np.where(n < MAX_EXACT, n, np.minimum(large, NUM_BUCKETS - 1))
    assert np.all(np.diff(bucket) >= 0) and bucket[MAX_DISTANCE - 1] == NUM_BUCKETS - 1
    return [int(n[bucket >= b].min()) for b in range(NUM_BUCKETS)]


BUCKET_START = _bucket_thresholds()


def _rms(x, g):
    ms = jnp.mean(x * x, axis=-1, keepdims=True)
    return x * lax.rsqrt(ms + EPS) * g


def _dot(a, b):
    return jnp.dot(a, b, preferred_element_type=F32)


def _dot_nt(a, b):
    return lax.dot_general(a, b, (((1,), (1,)), ((), ())), preferred_element_type=F32)


def _bias_from_distance(n, value_of_bucket):
    val = jnp.zeros(n.shape, F32) + value_of_bucket(0)
    for b in range(1, NUM_BUCKETS):
        val = jnp.where(n >= BUCKET_START[b], value_of_bucket(b), val)
    return val


def _lambda_full(lam_ref):
    s1 = jnp.sum(lam_ref[0:1, :] * lam_ref[1:2, :], axis=-1, keepdims=True)
    s2 = jnp.sum(lam_ref[2:3, :] * lam_ref[3:4, :], axis=-1, keepdims=True)
    return jnp.exp(s1) - jnp.exp(s2) + LAMBDA_INIT


def _inproj_kernel(x_ref, g1_ref, win_ref, lng_ref, lnb_ref, ws_ref, bs_ref, *out_refs, cs, prompt):
    if prompt:
        a_ref, q_ref, kf_ref, vf_ref, kb_ref, vt_ref = out_refs
    else:
        a_ref, q_ref, kf_ref, vf_ref, vn_ref = out_refs
    tm = x_ref.shape[0]
    xn = _rms(x_ref[...], g1_ref[...]).astype(BF16)

    q = _dot(xn, win_ref[:, 2 * D_A:2 * D_A + D_QK])
    if prompt:
        q_ref[0] = (q * (SCALE * LOG2E)).T.astype(BF16)
    else:
        q_ref[...] = (q * SCALE).astype(BF16)
    k = _dot(xn, win_ref[:, 2 * D_A + D_QK:2 * D_A + 2 * D_QK])
    vb = _dot(xn, win_ref[:, 2 * D_A + 2 * D_QK:])
    for hh in range(H_B):
        kf_ref[pl.ds(hh, tm, stride=H_B), :] = k[:, hh * LANES:(hh + 1) * LANES]
        vf_ref[pl.ds(hh, tm, stride=H_B), :] = vb[:, hh * LANES:(hh + 1) * LANES]
    if prompt:
        kb_ref[...] = k.astype(BF16)
        vt_ref[0] = vb.T.astype(BF16)

    u = _dot(xn, win_ref[:, 0:D_A])
    v = _dot(xn, win_ref[:, D_A:2 * D_A])
    row = lax.broadcasted_iota(jnp.int32, (cs, cs), 0)
    col = lax.broadcasted_iota(jnp.int32, (cs, cs), 1)
    for g in range(G_A):
        gs = slice(g * CH_A, (g + 1) * CH_A)
        vg = v[:, gs]
        mu = jnp.mean(vg, axis=-1, keepdims=True)
        d = vg - mu
        var = jnp.mean(d * d, axis=-1, keepdims=True)
        vn = d * lax.rsqrt(var + EPS) * lng_ref[:, gs] + lnb_ref[:, gs]
        if not prompt:
            vn_ref[:, gs] = vn
        vn16 = vn.astype(BF16)
        w = jnp.where(row >= col, ws_ref[g], 0.0).astype(BF16)
        for c in range(tm // cs):
            rs = slice(c * cs, (c + 1) * cs)
            s = _dot(w, vn16[rs, :]) + bs_ref[:, gs]
            a_ref[rs, gs] = (u[rs, gs] * s).astype(BF16)


def _in_proj(x, g1, win16, lng, lnb, ws, bs_full, *, tm, cs, seq=None):
    t, d = x.shape
    d_in = win16.shape[1]
    const = lambda i: (0, 0)
    tile = lambda w: pl.BlockSpec((tm, w), lambda i: (i, 0))
    heads = pl.BlockSpec((tm * H_B, LANES), lambda i: (i, 0))
    if seq is not None:
        tiles = seq // tm
        transposed = pl.BlockSpec((1, D_QK, tm), lambda i: (i // tiles, 0, i % tiles))
        transposed_shape = jax.ShapeDtypeStruct((t // seq, D_QK, seq), BF16)
    out_shape = [jax.ShapeDtypeStruct((t, D_A), BF16),
                 jax.ShapeDtypeStruct((t, D_QK), BF16) if seq is None else transposed_shape,
                 jax.ShapeDtypeStruct((t * H_B, LANES), F32), jax.ShapeDtypeStruct((t * H_B, LANES), F32)]
    out_specs = [tile(D_A), tile(D_QK) if seq is None else transposed, heads, heads]
    if seq is not None:
        out_shape += [jax.ShapeDtypeStruct((t, D_QK), BF16), transposed_shape]
        out_specs += [tile(D_QK), transposed]
    else:
        out_shape.append(jax.ShapeDtypeStruct((t, D_A), F32))
        out_specs.append(tile(D_A))
    return pl.pallas_call(
        functools.partial(_inproj_kernel, cs=cs, prompt=seq is not None),
        grid=(t // tm,),
        in_specs=[tile(d), pl.BlockSpec((1, d), const), pl.BlockSpec((d, d_in), const),
                  pl.BlockSpec((1, D_A), const), pl.BlockSpec((1, D_A), const),
                  pl.BlockSpec((G_A, cs, cs), lambda i: (0, 0, 0)), pl.BlockSpec((cs, D_A), const)],
        out_specs=out_specs,
        out_shape=out_shape,
        compiler_params=pltpu.CompilerParams(dimension_semantics=("arbitrary",),
                                             vmem_limit_bytes=VMEM_LIMIT),
        name="in_proj",
    )(x, g1, win16, lng, lnb, ws, bs_full)


def _attn_kernel(rb_ref, q_ref, k_ref, vt_ref, lam_ref, sg_ref, o_ref,
                 diag_scr, corner_scr, qzt_scr, vt1_scr, sa_scr, sb_scr, mxa_scr, mxb_scr, m_scr, acc_scr, *, t):
    b, hg, qi = pl.program_id(0), pl.program_id(1), pl.program_id(2)
    hp = qzt_scr.shape[0]
    c = LANES

    @pl.when((b == 0) & (hg == 0) & (qi == 0))
    def _build_bias_tiles():
        d0 = lax.broadcasted_iota(jnp.int32, (t, t), 1) - lax.broadcasted_iota(jnp.int32, (t, t), 0)
        dc = c + lax.broadcasted_iota(jnp.int32, (c, c), 1) - lax.broadcasted_iota(jnp.int32, (c, c), 0)
        for hh in range(H_B):
            near = _bias_from_distance(jnp.maximum(d0, 0), lambda bk: rb_ref[bk, hh]) * LOG2E
            diag_scr[hh] = jnp.where(d0 >= 0, near, NEG)
            corner_scr[hh] = (_bias_from_distance(dc, lambda bk: rb_ref[bk, hh])
                              - rb_ref[NUM_BUCKETS - 1, hh]) * LOG2E

    depth = lax.broadcasted_iota(jnp.int32, (LANES, t), 0)
    for e in range(hp):
        qt = q_ref[0, e * LANES:(e + 1) * LANES, :]
        zero = jnp.zeros_like(qt)
        qzt_scr[e] = jnp.concatenate([jnp.where(depth < DH, qt, zero), jnp.where(depth >= DH, qt, zero)], axis=1)
        m_scr[e] = jnp.full(m_scr.shape[1:], -jnp.inf, F32)
        acc_scr[e] = jnp.zeros(acc_scr.shape[1:], F32)

    @pl.when(qi == 0)
    def _stage_values():
        for e in range(hp):
            vt1_scr[e, 0:DV, :] = vt_ref[0, e * DV:(e + 1) * DV, :]
            vt1_scr[e, DV:DV + ONES, :] = jnp.ones((ONES, vt1_scr.shape[2]), BF16)

    def issue(buf, kj):
        s_scr, mx_scr = buf
        for e in range(hp):
            k = k_ref[0, pl.ds(pl.multiple_of(kj * t, t), t), e * LANES:(e + 1) * LANES]
            s = _dot(k, qzt_scr[e])
            s_scr[e] = s
            mx_scr[e] = jnp.max(s, axis=0, keepdims=True)

    def accumulate(e, s, mx, shift, kj):
        m_old = m_scr[e]
        m_new = jnp.maximum(m_old, mx + shift)
        alpha = jnp.exp2(m_old - m_new)
        p = jnp.exp2((s - (m_new - shift)).astype(BF16))
        vt1 = vt1_scr[e, :, pl.ds(pl.multiple_of(kj * t, t), t)]
        acc_scr[e] = alpha * acc_scr[e] + _dot(vt1, p)
        m_scr[e] = m_new

    def consume(buf, kj):
        s_scr, mx_scr = buf
        near = (kj == qi - 1).astype(F32)
        for e in range(hp):
            hh = hg * hp + e
            fix = near * corner_scr[hh]
            c0 = s_scr[e, t - c:t, 0:c] + fix
            c1 = s_scr[e, t - c:t, t:t + c] + fix
            s_scr[e, t - c:t, 0:c] = c0
            s_scr[e, t - c:t, t:t + c] = c1
            mx = mx_scr[e]
            mx = jnp.concatenate([jnp.maximum(mx[:, 0:c], jnp.max(c0, axis=0, keepdims=True)), mx[:, c:t],
                                  jnp.maximum(mx[:, t:t + c], jnp.max(c1, axis=0, keepdims=True)),
                                  mx[:, t + c:]], axis=1)
            accumulate(e, s_scr[e], mx, rb_ref[NUM_BUCKETS - 1, hh] * LOG2E, kj)

    buf_a, buf_b = (sa_scr, mxa_scr), (sb_scr, mxb_scr)
    issue(buf_a, 0)

    def block_pair(i, carry):
        issue(buf_b, 2 * i + 1)
        consume(buf_a, 2 * i)
        issue(buf_a, 2 * i + 2)
        consume(buf_b, 2 * i + 1)
        return carry

    lax.fori_loop(0, qi // 2, block_pair, 0)

    @pl.when(qi % 2 == 1)
    def _odd_block():
        issue(buf_b, qi)
        consume(buf_a, qi - 1)
        sa_scr[...] = sb_scr[...]

    for e in range(hp):
        bias = diag_scr[hg * hp + e]
        s = sa_scr[e]
        s = jnp.concatenate([s[:, 0:t] + bias, s[:, t:2 * t] + bias], axis=1)
        accumulate(e, s, jnp.max(s, axis=0, keepdims=True), 0.0, qi)

    lam = _lambda_full(lam_ref)
    for e in range(hp):
        ot = acc_scr[e, 0:DV, :] / acc_scr[e, DV:DV + 1, :]
        ot = ot[:, 0:t] - lam * ot[:, t:2 * t]
        ms = jnp.mean(ot * ot, axis=0, keepdims=True)
        o_ref[0, e * DV:(e + 1) * DV, :] = (ot * lax.rsqrt(ms + EPS) * sg_ref[...]
                                            * (1.0 - LAMBDA_INIT)).astype(o_ref.dtype)


def _prompt_attention(rel_bias, qt, k, vt, lam, sg_col, *, t, hp):
    b, s, _ = k.shape
    const = lambda bi, hi, qi: (0, 0)
    w = hp * LANES
    return pl.pallas_call(
        functools.partial(_attn_kernel, t=t),
        grid=(b, H_B // hp, s // t),
        in_specs=[pl.BlockSpec(memory_space=pltpu.SMEM),
                  pl.BlockSpec((1, w, t), lambda bi, hi, qi: (bi, hi, qi)),
                  pl.BlockSpec((1, s, w), lambda bi, hi, qi: (bi, 0, hi)),
                  pl.BlockSpec((1, w, s), lambda bi, hi, qi: (bi, hi, 0)),
                  pl.BlockSpec((4, DH), const), pl.BlockSpec((DV, 1), const)],
        out_specs=pl.BlockSpec((1, w, t), lambda bi, hi, qi: (bi, hi, qi)),
        out_shape=jax.ShapeDtypeStruct((b, D_B, s), BF16),
        scratch_shapes=[pltpu.VMEM((H_B, t, t), F32), pltpu.VMEM((H_B, LANES, LANES), F32),
                        pltpu.VMEM((hp, LANES, 2 * t), BF16), pltpu.VMEM((hp, DV + ONES, s), BF16),
                        pltpu.VMEM((hp, t, 2 * t), F32), pltpu.VMEM((hp, t, 2 * t), F32),
                        pltpu.VMEM((hp, 1, 2 * t), F32), pltpu.VMEM((hp, 1, 2 * t), F32),
                        pltpu.VMEM((hp, 1, 2 * t), F32), pltpu.VMEM((hp, DV + ONES, 2 * t), F32)],
        compiler_params=pltpu.CompilerParams(dimension_semantics=("arbitrary",) * 3,
                                             vmem_limit_bytes=VMEM_LIMIT),
        name="prompt_attn",
    )(rel_bias, qt, k, vt, lam, sg_col)


def _paged_kernel(pt_ref, rb_ref, q_ref, kn_ref, vn_ref, lam_ref, sg_ref, *rest, pages, dec):
    k_refs, v_refs = rest[:pages], rest[pages:2 * pages]
    o_ref, qh_scr, new_scr, tile_scr, far_scr, m_scr, l_scr, acc_scr = rest[2 * pages:]
    del pt_ref
    b, st = pl.program_id(0), pl.program_id(1)
    n_st = pl.num_programs(1)
    hrows = 2 * dec
    rows = H_B * hrows

    @pl.when((b == 0) & (st == 0))
    def _build_bias_tiles():
        r = lax.broadcasted_iota(jnp.int32, (rows, PAGE), 0)
        j = lax.broadcasted_iota(jnp.int32, (rows, PAGE), 1)
        head, i = r // hrows, r % dec

        def row_value(bk):
            val = jnp.zeros((rows, PAGE), F32) + rb_ref[bk, H_B - 1]
            for hh in range(H_B - 2, -1, -1):
                val = jnp.where(head == hh, rb_ref[bk, hh], val)
            return val

        tile_scr[0] = _bias_from_distance(PAGE + i - j, row_value)
        tile_scr[1] = jnp.where(j <= i, _bias_from_distance(jnp.maximum(i - j, 0), row_value), NEG)
        far_scr[...] = row_value(NUM_BUCKETS - 1)[:, 0:1]
        new_scr[...] = jnp.zeros_like(new_scr)

    @pl.when(st == 0)
    def _start_batch():
        r = lax.broadcasted_iota(jnp.int32, (hrows, LANES), 0)
        lane = lax.broadcasted_iota(jnp.int32, (hrows, LANES), 1)
        for hh in range(H_B):
            qh = q_ref[0, :, hh * LANES:(hh + 1) * LANES].astype(F32)
            qh = jnp.concatenate([qh, qh], axis=0)
            qh_scr[hh] = jnp.where(lane // DH == r // dec, qh, 0.0).astype(BF16)
        m_scr[...] = jnp.full_like(m_scr, -jnp.inf)
        l_scr[...] = jnp.zeros_like(l_scr)
        acc_scr[...] = jnp.zeros_like(acc_scr)

    def scores(head_keys):
        return jnp.concatenate([_dot_nt(qh_scr[hh], head_keys(hh)) for hh in range(H_B)], axis=0)

    def update(s, shift, head_values):
        m_old = m_scr[...]
        m_new = jnp.maximum(m_old, jnp.max(s, axis=-1, keepdims=True) + shift)
        alpha = jnp.exp(m_old - m_new)
        p = jnp.exp(s - (m_new - shift))
        l_scr[...] = alpha * l_scr[...] + jnp.sum(p, axis=-1, keepdims=True)
        p = p.astype(BF16)
        pv = jnp.concatenate([_dot(p[hh * hrows:(hh + 1) * hrows], head_values(hh))
                              for hh in range(H_B)], axis=0)
        acc_scr[...] = alpha * acc_scr[...] + pv
        m_scr[...] = m_new

    def cached(refs, first, last):
        return lambda hh: jnp.concatenate(
            [refs[p][0, pl.ds(hh, PAGE, stride=H_B), :] for p in range(first, last)], axis=0).astype(BF16)

    def fresh(which):
        return lambda hh: new_scr[which, :, hh * LANES:(hh + 1) * LANES].astype(BF16)

    @pl.when(st < n_st - 1)
    def _far_pages():
        update(scores(cached(k_refs, 0, pages)), far_scr[...], cached(v_refs, 0, pages))

    @pl.when(st == n_st - 1)
    def _last_pages_and_new_rows():
        if pages > 1:
            update(scores(cached(k_refs, 0, pages - 1)), far_scr[...], cached(v_refs, 0, pages - 1))
        update(scores(cached(k_refs, pages - 1, pages)) + tile_scr[0], 0.0,
               cached(v_refs, pages - 1, pages))
        for hh in range(H_B):
            new_scr[0, 0:dec, hh * LANES:(hh + 1) * LANES] = kn_ref[0, pl.ds(hh, dec, stride=H_B), :]
            new_scr[1, 0:dec, hh * LANES:(hh + 1) * LANES] = vn_ref[0, pl.ds(hh, dec, stride=H_B), :]
        update(scores(fresh(0)) + tile_scr[1], 0.0, fresh(1))

        o = acc_scr[...] / l_scr[...]
        lam = _lambda_full(lam_ref)
        for hh in range(H_B):
            r0 = hh * hrows
            oh = o[r0:r0 + dec, :] - lam * o[r0 + dec:r0 + hrows, :]
            o_ref[0, :, hh * DV:(hh + 1) * DV] = (_rms(oh, sg_ref[...])
                                                  * (1.0 - LAMBDA_INIT)).astype(o_ref.dtype)


def _sample_attention(page_table, rel_bias, q, k_new, v_new, cache_k, cache_v, lam, sg, *, pages):
    nb, dec, _ = q.shape
    n_pages = page_table.shape[1]
    n_st = n_pages // pages
    rows = 2 * H_B * dec
    const2 = lambda bi, si, pt: (0, 0)
    per_batch = pl.BlockSpec((1, dec, D_QK), lambda bi, si, pt: (bi, 0, 0))
    new_rows = pl.BlockSpec((1, dec * H_B, LANES), lambda bi, si, pt: (bi, 0, 0))

    def page_spec(p):
        return pl.BlockSpec((1, PAGE * H_B, LANES),
                            lambda bi, si, pt: (pt[bi * n_pages + si * pages + p], 0, 0))

    grid_spec = pltpu.PrefetchScalarGridSpec(
        num_scalar_prefetch=1,
        grid=(nb, n_st),
        in_specs=[pl.BlockSpec(memory_space=pltpu.SMEM), per_batch, new_rows, new_rows,
                  pl.BlockSpec((4, DH), const2), pl.BlockSpec((1, DV), const2)]
                 + [page_spec(p) for p in range(pages)] * 2,
        out_specs=per_batch,
        scratch_shapes=[pltpu.VMEM((H_B, 2 * dec, LANES), BF16), pltpu.VMEM((2, PAGE, D_QK), F32),
                        pltpu.VMEM((2, rows, PAGE), F32), pltpu.VMEM((rows, 1), F32),
                        pltpu.VMEM((rows, 1), F32), pltpu.VMEM((rows, 1), F32),
                        pltpu.VMEM((rows, DV), F32)],
    )
    return pl.pallas_call(
        functools.partial(_paged_kernel, pages=pages, dec=dec),
        grid_spec=grid_spec,
        out_shape=jax.ShapeDtypeStruct((nb, dec, D_B), BF16),
        compiler_params=pltpu.CompilerParams(dimension_semantics=("arbitrary", "arbitrary"),
                                             vmem_limit_bytes=VMEM_LIMIT),
        name="paged_attn",
    )(page_table.reshape(-1), rel_bias, q, k_new, v_new, lam, sg,
      *([cache_k] * pages), *([cache_v] * pages))


def _tail_kernel(x_ref, a_ref, b_ref, wo_ref, g2_ref, wg_ref, wu_ref, wd_ref, gf_ref, y_ref, *, b_transposed):
    if b_transposed:
        bw = lax.dot_general(b_ref[0], wo_ref[D_A:, :], (((0,), (0,)), ((), ())), preferred_element_type=F32)
    else:
        bw = _dot(b_ref[...], wo_ref[D_A:, :])
    h = x_ref[...] + _dot(a_ref[...], wo_ref[0:D_A, :]) + bw
    hn = _rms(h, g2_ref[...]).astype(BF16)
    gate = _dot(hn, wg_ref[...])
    f = (gate * (1.0 / (1.0 + jnp.exp(-gate))) * _dot(hn, wu_ref[...])).astype(BF16)
    y_ref[...] = _rms(h + _dot(f, wd_ref[...]), gf_ref[...])


def _tail(x, a, bmix, wo16, g2, wg16, wu16, wd16, gf, *, tm):
    t, d = x.shape
    d_ff = wg16.shape[1]
    const = lambda i: (0, 0)
    resident = lambda shape: pl.BlockSpec(shape, const, pipeline_mode=pl.Buffered(1))
    b_transposed = bmix.ndim == 3
    if b_transposed:
        tiles = bmix.shape[2] // tm
        b_spec = pl.BlockSpec((1, D_B, tm), lambda i: (i // tiles, 0, i % tiles))
    else:
        b_spec = pl.BlockSpec((tm, D_B), lambda i: (i, 0))
    return pl.pallas_call(
        functools.partial(_tail_kernel, b_transposed=b_transposed),
        grid=(t // tm,),
        in_specs=[pl.BlockSpec((tm, d), lambda i: (i, 0)), pl.BlockSpec((tm, D_A), lambda i: (i, 0)),
                  b_spec, resident((D_A + D_B, d)),
                  pl.BlockSpec((1, d), const), resident((d, d_ff)), resident((d, d_ff)),
                  resident((d_ff, d)), pl.BlockSpec((1, d), const)],
        out_specs=pl.BlockSpec((tm, d), lambda i: (i, 0)),
        out_shape=jax.ShapeDtypeStruct((t, d), F32),
        compiler_params=pltpu.CompilerParams(dimension_semantics=("arbitrary",),
                                             vmem_limit_bytes=VMEM_LIMIT),
        name="tail",
    )(x, a, bmix, wo16, g2, wg16, wu16, wd16, gf)


def kernel(x_prompt, x_sample, cache_k, cache_v, page_table, norm1_g, w_in, gmlp_ln_g, gmlp_ln_b, gmlp_ws, gmlp_bs, lambda_q1, lambda_k1, lambda_q2, lambda_k2, subln_g, rel_bias, w_out, norm2_g, w_gate, w_up, w_down, final_g):
    nb, seq, d = x_prompt.shape
    db, dec, _ = x_sample.shape
    row = lambda p: p.reshape(1, -1)

    win16, wo16 = w_in.astype(BF16), w_out.astype(BF16)
    wg16, wu16, wd16 = w_gate.astype(BF16), w_up.astype(BF16), w_down.astype(BF16)
    lam = jnp.stack([lambda_q1, lambda_k1, lambda_q2, lambda_k2])
    proj_params = (row(norm1_g), win16, row(gmlp_ln_g), row(gmlp_ln_b))
    tail_params = (wo16, row(norm2_g), wg16, wu16, wd16, row(final_g))

    xp = x_prompt.reshape(nb * seq, d)
    bs_p = jnp.repeat(gmlp_bs.T, CH_A, axis=1)
    a_p, qt_p, kf_p, vf_p, kb_p, vt_p = _in_proj(xp, *proj_params, gmlp_ws, bs_p,
                                                tm=1024, cs=CHUNK, seq=seq)
    bt_p = _prompt_attention(rel_bias, qt_p, kb_p.reshape(nb, seq, D_QK), vt_p, lam,
                             subln_g.reshape(DV, 1), t=512, hp=2)
    y_p = _tail(xp, a_p, bt_p, *tail_params, tm=512)

    xs = x_sample.reshape(db * dec, d)
    ws_s = jnp.einsum('ab,gts->gatbs', jnp.eye(db, dtype=F32),
                      gmlp_ws[:, :dec, :dec]).reshape(G_A, db * dec, db * dec)
    bs_s = jnp.tile(jnp.repeat(gmlp_bs[:, :dec].T, CH_A, axis=1), (db, 1))
    a_s, q_s, kf_s, vf_s, gv_s = _in_proj(xs, *proj_params, ws_s, bs_s, tm=db * dec, cs=db * dec)
    b_s = _sample_attention(page_table, rel_bias, q_s.reshape(db, dec, D_QK),
                            kf_s.reshape(db, dec * H_B, LANES), vf_s.reshape(db, dec * H_B, LANES),
                            cache_k.reshape(-1, PAGE * H_B, DV), cache_v.reshape(-1, PAGE * H_B, DV),
                            lam, row(subln_g), pages=32)
    y_s = _tail(xs, a_s, b_s.reshape(db * dec, D_B), *tail_params, tm=db * dec)

    return (y_p.reshape(nb, seq, d), y_s.reshape(db, dec, d),
            kf_p.reshape(nb, seq, H_B, 2 * DH), vf_p.reshape(nb, seq, H_B, DV),
            kf_s.reshape(db, dec, H_B, 2 * DH), vf_s.reshape(db, dec, H_B, DV),
            gv_s.reshape(db, dec, D_A))
```

```python
import functools
import math

import numpy as np
import jax
import jax.numpy as jnp
from jax import lax
from jax.experimental import pallas as pl
from jax.experimental.pallas import tpu as pltpu

F32 = jnp.float32
BF16 = jnp.bfloat16

EPS = 1e-6
LANES = 128
G_A = 4
CH_A = 128
CHUNK = 128
H_B = 4
DV = 128
DH = 64
D_A = G_A * CH_A
D_QK = H_B * 2 * DH
D_B = H_B * DV
SCALE = DH ** -0.5
LOG2E = math.log2(math.e)
LAMBDA_INIT = 0.8 - 0.6 * math.exp(-0.3 * 0)
NUM_BUCKETS = 32
MAX_EXACT = NUM_BUCKETS // 2
MAX_DISTANCE = 128
PAGE = 128
ONES = 16
NEG = -1e30
VMEM_LIMIT = 56 * 1024 * 1024


def _bucket_thresholds():
    n = np.arange(0, 2 * MAX_DISTANCE)
    nf = np.maximum(n, 1).astype(np.float64)
    large = MAX_EXACT + (np.log(nf / MAX_EXACT) / math.log(MAX_DISTANCE / MAX_EXACT)
                         * (NUM_BUCKETS - MAX_EXACT)).astype(np.int32)
    bucket = np.where(n < MAX_EXACT, n, np.minimum(large, NUM_BUCKETS - 1))
    assert np.all(np.diff(bucket) >= 0) and bucket[MAX_DISTANCE - 1] == NUM_BUCKETS - 1
    return [int(n[bucket >= b].min()) for b in range(NUM_BUCKETS)]


BUCKET_START = _bucket_thresholds()


def _rms(x, g):
    ms = jnp.mean(x * x, axis=-1, keepdims=True)
    return x * lax.rsqrt(ms + EPS) * g


def _dot(a, b):
    return jnp.dot(a, b, preferred_element_type=F32)


def _dot_nt(a, b):
    return lax.dot_general(a, b, (((1,), (1,)), ((), ())), preferred_element_type=F32)


def _bias_from_distance(n, value_of_bucket):
    val = jnp.zeros(n.shape, F32) + value_of_bucket(0)
    for b in range(1, NUM_BUCKETS):
        val = jnp.where(n >= BUCKET_START[b], value_of_bucket(b), val)
    return val


def _lambda_full(lam_ref):
    s1 = jnp.sum(lam_ref[0:1, :] * lam_ref[1:2, :], axis=-1, keepdims=True)
    s2 = jnp.sum(lam_ref[2:3, :] * lam_ref[3:4, :], axis=-1, keepdims=True)
    return jnp.exp(s1) - jnp.exp(s2) + LAMBDA_INIT


def _inproj_kernel(x_ref, g1_ref, win_ref, lng_ref, lnb_ref, ws_ref, bs_ref, *out_refs, cs, prompt):
    if prompt:
        a_ref, q_ref, kf_ref, vf_ref, kb_ref, vt_ref = out_refs
    else:
        a_ref, q_ref, kf_ref, vf_ref, vn_ref = out_refs
    tm = x_ref.shape[0]
    xn = _rms(x_ref[...], g1_ref[...]).astype(BF16)

    q = _dot(xn, win_ref[:, 2 * D_A:2 * D_A + D_QK])
    if prompt:
        q_ref[0] = (q * (SCALE * LOG2E)).T.astype(BF16)
    else:
        q_ref[...] = (q * SCALE).astype(BF16)
    k = _dot(xn, win_ref[:, 2 * D_A + D_QK:2 * D_A + 2 * D_QK])
    vb = _dot(xn, win_ref[:, 2 * D_A + 2 * D_QK:])
    for hh in range(H_B):
        kf_ref[pl.ds(hh, tm, stride=H_B), :] = k[:, hh * LANES:(hh + 1) * LANES]
        vf_ref[pl.ds(hh, tm, stride=H_B), :] = vb[:, hh * LANES:(hh + 1) * LANES]
    if prompt:
        kb_ref[...] = k.astype(BF16)
        vt_ref[0] = vb.T.astype(BF16)

    u = _dot(xn, win_ref[:, 0:D_A])
    v = _dot(xn, win_ref[:, D_A:2 * D_A])
    row = lax.broadcasted_iota(jnp.int32, (cs, cs), 0)
    col = lax.broadcasted_iota(jnp.int32, (cs, cs), 1)
    for g in range(G_A):
        gs = slice(g * CH_A, (g + 1) * CH_A)
        vg = v[:, gs]
        mu = jnp.mean(vg, axis=-1, keepdims=True)
        d = vg - mu
        var = jnp.mean(d * d, axis=-1, keepdims=True)
        vn = d * lax.rsqrt(var + EPS) * lng_ref[:, gs] + lnb_ref[:, gs]
        if not prompt:
            vn_ref[:, gs] = vn
        vn16 = vn.astype(BF16)
        w = jnp.where(row >= col, ws_ref[g], 0.0).astype(BF16)
        for c in range(tm // cs):
            rs = slice(c * cs, (c + 1) * cs)
            s = _dot(w, vn16[rs, :]) + bs_ref[:, gs]
            a_ref[rs, gs] = (u[rs, gs] * s).astype(BF16)


def _in_proj(x, g1, win16, lng, lnb, ws, bs_full, *, tm, cs, seq=None):
    t, d = x.shape
    d_in = win16.shape[1]
    const = lambda i: (0, 0)
    tile = lambda w: pl.BlockSpec((tm, w), lambda i: (i, 0))
    heads = pl.BlockSpec((tm * H_B, LANES), lambda i: (i, 0))
    if seq is not None:
        tiles = seq // tm
        transposed = pl.BlockSpec((1, D_QK, tm), lambda i: (i // tiles, 0, i % tiles))
        transposed_shape = jax.ShapeDtypeStruct((t // seq, D_QK, seq), BF16)
    out_shape = [jax.ShapeDtypeStruct((t, D_A), BF16),
                 jax.ShapeDtypeStruct((t, D_QK), BF16) if seq is None else transposed_shape,
                 jax.ShapeDtypeStruct((t * H_B, LANES), F32), jax.ShapeDtypeStruct((t * H_B, LANES), F32)]
    out_specs = [tile(D_A), tile(D_QK) if seq is None else transposed, heads, heads]
    if seq is not None:
        out_shape += [jax.ShapeDtypeStruct((t, D_QK), BF16), transposed_shape]
        out_specs += [tile(D_QK), transposed]
    else:
        out_shape.append(jax.ShapeDtypeStruct((t, D_A), F32))
        out_specs.append(tile(D_A))
    return pl.pallas_call(
        functools.partial(_inproj_kernel, cs=cs, prompt=seq is not None),
        grid=(t // tm,),
        in_specs=[tile(d), pl.BlockSpec((1, d), const), pl.BlockSpec((d, d_in), const),
                  pl.BlockSpec((1, D_A), const), pl.BlockSpec((1, D_A), const),
                  pl.BlockSpec((G_A, cs, cs), lambda i: (0, 0, 0)), pl.BlockSpec((cs, D_A), const)],
        out_specs=out_specs,
        out_shape=out_shape,
        compiler_params=pltpu.CompilerParams(dimension_semantics=("arbitrary",),
                                             vmem_limit_bytes=VMEM_LIMIT),
        name="in_proj",
    )(x, g1, win16, lng, lnb, ws, bs_full)


def _attn_kernel(rb_ref, q_ref, k_ref, vt_ref, lam_ref, sg_ref, o_ref,
                 diag_scr, corner_scr, qzt_scr, vt1_scr, sa_scr, sb_scr, sc_scr, mxa_scr, mxb_scr, mxc_scr,
                 m_scr, acc_scr, *, t):
    b, hg = pl.program_id(0), pl.program_id(1)
    hp = vt1_scr.shape[0]
    nq = k_ref.shape[1] // t
    c = LANES

    @pl.when((b == 0) & (hg == 0))
    def _build_bias_tiles():
        d0 = lax.broadcasted_iota(jnp.int32, (t, t), 1) - lax.broadcasted_iota(jnp.int32, (t, t), 0)
        dc = c + lax.broadcasted_iota(jnp.int32, (c, c), 1) - lax.broadcasted_iota(jnp.int32, (c, c), 0)
        for hh in range(H_B):
            near = _bias_from_distance(jnp.maximum(d0, 0), lambda bk: rb_ref[bk, hh]) * LOG2E
            diag_scr[hh] = jnp.where(d0 >= 0, near, NEG)
            corner_scr[hh] = (_bias_from_distance(dc, lambda bk: rb_ref[bk, hh])
                              - rb_ref[NUM_BUCKETS - 1, hh]) * LOG2E

    for e in range(hp):
        vt1_scr[e, 0:DV, :] = vt_ref[0, e * DV:(e + 1) * DV, :]
        vt1_scr[e, DV:DV + ONES, :] = jnp.ones((ONES, vt1_scr.shape[2]), BF16)

    def stage_queries(slot, tile):
        depth = lax.broadcasted_iota(jnp.int32, (LANES, t), 0)
        for e in range(hp):
            qt = q_ref[0, e * LANES:(e + 1) * LANES, pl.ds(pl.multiple_of(tile * t, t), t)]
            zero = jnp.zeros_like(qt)
            qzt_scr[slot, e] = jnp.concatenate([jnp.where(depth < DH, qt, zero),
                                                jnp.where(depth >= DH, qt, zero)], axis=1)

    def reset_state():
        for e in range(hp):
            m_scr[e] = jnp.full(m_scr.shape[1:], -jnp.inf, F32)
            acc_scr[e] = jnp.zeros(acc_scr.shape[1:], F32)

    def issue(buf, slot, kj):
        s_scr, mx_scr = buf
        for e in range(hp):
            k = k_ref[0, pl.ds(pl.multiple_of(kj * t, t), t), e * LANES:(e + 1) * LANES]
            s = _dot(k, qzt_scr[slot, e])
            s_scr[e] = s
            mx_scr[e] = jnp.max(s, axis=0, keepdims=True)

    def accumulate(e, s, mx, shift, kj):
        m_old = m_scr[e]
        m_new = jnp.maximum(m_old, mx + shift)
        alpha = jnp.exp2(m_old - m_new)
        p = jnp.exp2((s - (m_new - shift)).astype(BF16))
        vt1 = vt1_scr[e, :, pl.ds(pl.multiple_of(kj * t, t), t)]
        acc_scr[e] = alpha * acc_scr[e] + _dot(vt1, p)
        m_scr[e] = m_new

    def consume(buf, kj, qi):
        s_scr, mx_scr = buf
        near = (kj == qi - 1).astype(F32)
        for e in range(hp):
            hh = hg * hp + e
            fix = near * corner_scr[hh]
            c0 = s_scr[e, t - c:t, 0:c] + fix
            c1 = s_scr[e, t - c:t, t:t + c] + fix
            s_scr[e, t - c:t, 0:c] = c0
            s_scr[e, t - c:t, t:t + c] = c1
            mx = mx_scr[e]
            mx = jnp.concatenate([jnp.maximum(mx[:, 0:c], jnp.max(c0, axis=0, keepdims=True)), mx[:, c:t],
                                  jnp.maximum(mx[:, t:t + c], jnp.max(c1, axis=0, keepdims=True)),
                                  mx[:, t + c:]], axis=1)
            accumulate(e, s_scr[e], mx, rb_ref[NUM_BUCKETS - 1, hh] * LOG2E, kj)

    def finish_tile(buf, qi):
        lam = _lambda_full(lam_ref)
        for e in range(hp):
            bias = diag_scr[hg * hp + e]
            s = buf[0][e]
            s = jnp.concatenate([s[:, 0:t] + bias, s[:, t:2 * t] + bias], axis=1)
            accumulate(e, s, jnp.max(s, axis=0, keepdims=True), 0.0, qi)
        for e in range(hp):
            ot = acc_scr[e, 0:DV, :] / acc_scr[e, DV:DV + 1, :]
            ot = ot[:, 0:t] - lam * ot[:, t:2 * t]
            ms = jnp.mean(ot * ot, axis=0, keepdims=True)
            o_ref[0, e * DV:(e + 1) * DV, pl.ds(pl.multiple_of(qi * t, t), t)] = (
                ot * lax.rsqrt(ms + EPS) * sg_ref[...] * (1.0 - LAMBDA_INIT)).astype(o_ref.dtype)
        reset_state()

    buf_a, buf_b, buf_c = (sa_scr, mxa_scr), (sb_scr, mxb_scr), (sc_scr, mxc_scr)

    def prefetch_next_tile(qi):
        stage_queries(1, jnp.minimum(qi + 1, nq - 1))
        issue(buf_c, 1, 0)

    def tile(qi, carry):
        @pl.when(qi == 0)
        def _first_tile():
            finish_tile(buf_c, qi)
            stage_queries(0, 1)
            issue(buf_c, 0, 0)

        @pl.when(qi > 0)
        def _first_block():
            issue(buf_a, 0, 1)
            consume(buf_c, 0, qi)

        def block_pair(i, inner):
            issue(buf_b, 0, 2 * i + 2)
            consume(buf_a, 2 * i + 1, qi)
            issue(buf_a, 0, 2 * i + 3)
            consume(buf_b, 2 * i + 2, qi)
            return inner

        lax.fori_loop(0, lax.shift_right_arithmetic(qi - 1, 1), block_pair, 0)

        @pl.when((qi > 0) & (qi % 2 == 1))
        def _diagonal_in_a():
            prefetch_next_tile(qi)
            finish_tile(buf_a, qi)
            qzt_scr[0] = qzt_scr[1]

        @pl.when((qi > 0) & (qi % 2 == 0))
        def _diagonal_in_b():
            issue(buf_b, 0, qi)
            consume(buf_a, qi - 1, qi)
            prefetch_next_tile(qi)
            finish_tile(buf_b, qi)
            qzt_scr[0] = qzt_scr[1]

        return carry

    stage_queries(0, 0)
    reset_state()
    issue(buf_c, 0, 0)
    lax.fori_loop(0, nq, tile, 0)


def _prompt_attention(rel_bias, qt, k, vt, lam, sg_col, *, t, hp):
    b, s, _ = k.shape
    const = lambda bi, hi: (0, 0)
    w = hp * LANES
    transposed = pl.BlockSpec((1, w, s), lambda bi, hi: (bi, hi, 0))
    scores = pltpu.VMEM((hp, t, 2 * t), F32)
    per_query = pltpu.VMEM((hp, 1, 2 * t), F32)
    return pl.pallas_call(
        functools.partial(_attn_kernel, t=t),
        grid=(b, H_B // hp),
        in_specs=[pl.BlockSpec(memory_space=pltpu.SMEM), transposed,
                  pl.BlockSpec((1, s, w), lambda bi, hi: (bi, 0, hi)), transposed,
                  pl.BlockSpec((4, DH), const), pl.BlockSpec((DV, 1), const)],
        out_specs=transposed,
        out_shape=jax.ShapeDtypeStruct((b, D_B, s), BF16),
        scratch_shapes=[pltpu.VMEM((H_B, t, t), F32), pltpu.VMEM((H_B, LANES, LANES), F32),
                        pltpu.VMEM((2, hp, LANES, 2 * t), BF16), pltpu.VMEM((hp, DV + ONES, s), BF16),
                        scores, scores, scores, per_query, per_query, per_query,
                        per_query, pltpu.VMEM((hp, DV + ONES, 2 * t), F32)],
        compiler_params=pltpu.CompilerParams(dimension_semantics=("arbitrary",) * 2,
                                             vmem_limit_bytes=VMEM_LIMIT),
        name="prompt_attn",
    )(rel_bias, qt, k, vt, lam, sg_col)


def _paged_kernel(pt_ref, rb_ref, q_ref, kn_ref, vn_ref, lam_ref, sg_ref, *rest, pages, dec):
    k_refs, v_refs = rest[:pages], rest[pages:2 * pages]
    o_ref, qh_scr, new_scr, tile_scr, far_scr, m_scr, l_scr, acc_scr = rest[2 * pages:]
    del pt_ref
    b, st = pl.program_id(0), pl.program_id(1)
    n_st = pl.num_programs(1)
    hrows = 2 * dec
    rows = H_B * hrows

    @pl.when((b == 0) & (st == 0))
    def _build_bias_tiles():
        r = lax.broadcasted_iota(jnp.int32, (rows, PAGE), 0)
        j = lax.broadcasted_iota(jnp.int32, (rows, PAGE), 1)
        head, i = r // hrows, r % dec

        def row_value(bk):
            val = jnp.zeros((rows, PAGE), F32) + rb_ref[bk, H_B - 1]
            for hh in range(H_B - 2, -1, -1):
                val = jnp.where(head == hh, rb_ref[bk, hh], val)
            return val

        tile_scr[0] = _bias_from_distance(PAGE + i - j, row_value)
        tile_scr[1] = jnp.where(j <= i, _bias_from_distance(jnp.maximum(i - j, 0), row_value), NEG)
        far_scr[...] = row_value(NUM_BUCKETS - 1)[:, 0:1]
        new_scr[...] = jnp.zeros_like(new_scr)

    @pl.when(st == 0)
    def _start_batch():
        r = lax.broadcasted_iota(jnp.int32, (hrows, LANES), 0)
        lane = lax.broadcasted_iota(jnp.int32, (hrows, LANES), 1)
        for hh in range(H_B):
            qh = q_ref[0, :, hh * LANES:(hh + 1) * LANES].astype(F32)
            qh = jnp.concatenate([qh, qh], axis=0)
            qh_scr[hh] = jnp.where(lane // DH == r // dec, qh, 0.0).astype(BF16)
        m_scr[...] = jnp.full_like(m_scr, -jnp.inf)
        l_scr[...] = jnp.zeros_like(l_scr)
        acc_scr[...] = jnp.zeros_like(acc_scr)

    def scores(head_keys):
        return jnp.concatenate([_dot_nt(qh_scr[hh], head_keys(hh)) for hh in range(H_B)], axis=0)

    def update(s, shift, head_values):
        m_old = m_scr[...]
        m_new = jnp.maximum(m_old, jnp.max(s, axis=-1, keepdims=True) + shift)
        alpha = jnp.exp(m_old - m_new)
        p = jnp.exp(s - (m_new - shift))
        l_scr[...] = alpha * l_scr[...] + jnp.sum(p, axis=-1, keepdims=True)
        p = p.astype(BF16)
        pv = jnp.concatenate([_dot(p[hh * hrows:(hh + 1) * hrows], head_values(hh))
                              for hh in range(H_B)], axis=0)
        acc_scr[...] = alpha * acc_scr[...] + pv
        m_scr[...] = m_new

    def cached(refs, first, last):
        return lambda hh: jnp.concatenate(
            [refs[p][0, pl.ds(hh, PAGE, stride=H_B), :] for p in range(first, last)], axis=0).astype(BF16)

    def fresh(which):
        return lambda hh: new_scr[which, :, hh * LANES:(hh + 1) * LANES].astype(BF16)

    @pl.when(st < n_st - 1)
    def _far_pages():
        update(scores(cached(k_refs, 0, pages)), far_scr[...], cached(v_refs, 0, pages))

    @pl.when(st == n_st - 1)
    def _last_pages_and_new_rows():
        if pages > 1:
            update(scores(cached(k_refs, 0, pages - 1)), far_scr[...], cached(v_refs, 0, pages - 1))
        update(scores(cached(k_refs, pages - 1, pages)) + tile_scr[0], 0.0,
               cached(v_refs, pages - 1, pages))
        for hh in range(H_B):
            new_scr[0, 0:dec, hh * LANES:(hh + 1) * LANES] = kn_ref[0, pl.ds(hh, dec, stride=H_B), :]
            new_scr[1, 0:dec, hh * LANES:(hh + 1) * LANES] = vn_ref[0, pl.ds(hh, dec, stride=H_B), :]
        update(scores(fresh(0)) + tile_scr[1], 0.0, fresh(1))

        o = acc_scr[...] / l_scr[...]
        lam = _lambda_full(lam_ref)
        for hh in range(H_B):
            r0 = hh * hrows
            oh = o[r0:r0 + dec, :] - lam * o[r0 + dec:r0 + hrows, :]
            o_ref[0, :, hh * DV:(hh + 1) * DV] = (_rms(oh, sg_ref[...])
                                                  * (1.0 - LAMBDA_INIT)).astype(o_ref.dtype)


def _sample_attention(page_table, rel_bias, q, k_new, v_new, cache_k, cache_v, lam, sg, *, pages):
    nb, dec, _ = q.shape
    n_pages = page_table.shape[1]
    n_st = n_pages // pages
    rows = 2 * H_B * dec
    const2 = lambda bi, si, pt: (0, 0)
    per_batch = pl.BlockSpec((1, dec, D_QK), lambda bi, si, pt: (bi, 0, 0))
    new_rows = pl.BlockSpec((1, dec * H_B, LANES), lambda bi, si, pt: (bi, 0, 0))

    def page_spec(p):
        return pl.BlockSpec((1, PAGE * H_B, LANES),
                            lambda bi, si, pt: (pt[bi * n_pages + si * pages + p], 0, 0))

    grid_spec = pltpu.PrefetchScalarGridSpec(
        num_scalar_prefetch=1,
        grid=(nb, n_st),
        in_specs=[pl.BlockSpec(memory_space=pltpu.SMEM), per_batch, new_rows, new_rows,
                  pl.BlockSpec((4, DH), const2), pl.BlockSpec((1, DV), const2)]
                 + [page_spec(p) for p in range(pages)] * 2,
        out_specs=per_batch,
        scratch_shapes=[pltpu.VMEM((H_B, 2 * dec, LANES), BF16), pltpu.VMEM((2, PAGE, D_QK), F32),
                        pltpu.VMEM((2, rows, PAGE), F32), pltpu.VMEM((rows, 1), F32),
                        pltpu.VMEM((rows, 1), F32), pltpu.VMEM((rows, 1), F32),
                        pltpu.VMEM((rows, DV), F32)],
    )
    return pl.pallas_call(
        functools.partial(_paged_kernel, pages=pages, dec=dec),
        grid_spec=grid_spec,
        out_shape=jax.ShapeDtypeStruct((nb, dec, D_B), BF16),
        compiler_params=pltpu.CompilerParams(dimension_semantics=("arbitrary", "arbitrary"),
                                             vmem_limit_bytes=VMEM_LIMIT),
        name="paged_attn",
    )(page_table.reshape(-1), rel_bias, q, k_new, v_new, lam, sg,
      *([cache_k] * pages), *([cache_v] * pages))


def _tail_kernel(x_ref, a_ref, b_ref, wo_ref, g2_ref, wg_ref, wu_ref, wd_ref, gf_ref, y_ref, *, b_transposed):
    if b_transposed:
        bw = lax.dot_general(b_ref[0], wo_ref[D_A:, :], (((0,), (0,)), ((), ())), preferred_element_type=F32)
    else:
        bw = _dot(b_ref[...], wo_ref[D_A:, :])
    h = x_ref[...] + _dot(a_ref[...], wo_ref[0:D_A, :]) + bw
    hn = _rms(h, g2_ref[...]).astype(BF16)
    gate = _dot(hn, wg_ref[...])
    f = (gate * (1.0 / (1.0 + jnp.exp(-gate))) * _dot(hn, wu_ref[...])).astype(BF16)
    y_ref[...] = _rms(h + _dot(f, wd_ref[...]), gf_ref[...])


def _tail(x, a, bmix, wo16, g2, wg16, wu16, wd16, gf, *, tm):
    t, d = x.shape
    d_ff = wg16.shape[1]
    const = lambda i: (0, 0)
    resident = lambda shape: pl.BlockSpec(shape, const, pipeline_mode=pl.Buffered(1))
    b_transposed = bmix.ndim == 3
    if b_transposed:
        tiles = bmix.shape[2] // tm
        b_spec = pl.BlockSpec((1, D_B, tm), lambda i: (i // tiles, 0, i % tiles))
    else:
        b_spec = pl.BlockSpec((tm, D_B), lambda i: (i, 0))
    return pl.pallas_call(
        functools.partial(_tail_kernel, b_transposed=b_transposed),
        grid=(t // tm,),
        in_specs=[pl.BlockSpec((tm, d), lambda i: (i, 0)), pl.BlockSpec((tm, D_A), lambda i: (i, 0)),
                  b_spec, resident((D_A + D_B, d)),
                  pl.BlockSpec((1, d), const), resident((d, d_ff)), resident((d, d_ff)),
                  resident((d_ff, d)), pl.BlockSpec((1, d), const)],
        out_specs=pl.BlockSpec((tm, d), lambda i: (i, 0)),
        out_shape=jax.ShapeDtypeStruct((t, d), F32),
        compiler_params=pltpu.CompilerParams(dimension_semantics=("arbitrary",),
                                             vmem_limit_bytes=VMEM_LIMIT),
        name="tail",
    )(x, a, bmix, wo16, g2, wg16, wu16, wd16, gf)


def kernel(x_prompt, x_sample, cache_k, cache_v, page_table, norm1_g, w_in, gmlp_ln_g, gmlp_ln_b, gmlp_ws, gmlp_bs, lambda_q1, lambda_k1, lambda_q2, lambda_k2, subln_g, rel_bias, w_out, norm2_g, w_gate, w_up, w_down, final_g):
    nb, seq, d = x_prompt.shape
    db, dec, _ = x_sample.shape
    row = lambda p: p.reshape(1, -1)

    win16, wo16 = w_in.astype(BF16), w_out.astype(BF16)
    wg16, wu16, wd16 = w_gate.astype(BF16), w_up.astype(BF16), w_down.astype(BF16)
    lam = jnp.stack([lambda_q1, lambda_k1, lambda_q2, lambda_k2])
    proj_params = (row(norm1_g), win16, row(gmlp_ln_g), row(gmlp_ln_b))
    tail_params = (wo16, row(norm2_g), wg16, wu16, wd16, row(final_g))

    xp = x_prompt.reshape(nb * seq, d)
    bs_p = jnp.repeat(gmlp_bs.T, CH_A, axis=1)
    a_p, qt_p, kf_p, vf_p, kb_p, vt_p = _in_proj(xp, *proj_params, gmlp_ws, bs_p,
                                                tm=1024, cs=CHUNK, seq=seq)
    bt_p = _prompt_attention(rel_bias, qt_p, kb_p.reshape(nb, seq, D_QK), vt_p, lam,
                             subln_g.reshape(DV, 1), t=512, hp=2)
    y_p = _tail(xp, a_p, bt_p, *tail_params, tm=512)

    xs = x_sample.reshape(db * dec, d)
    pos = np.arange(db * dec)
    pick = jnp.asarray(pos[:, None] % dec == np.arange(dec)[None, :], F32)
    same = jnp.asarray(pos[:, None] // dec == pos[None, :] // dec)
    ws_s = jnp.where(same, jnp.einsum('rt,gts,cs->grc', pick, gmlp_ws[:, :dec, :dec], pick,
                                      precision=lax.Precision.HIGHEST), 0.0)
    bs_s = jnp.tile(jnp.repeat(gmlp_bs[:, :dec].T, CH_A, axis=1), (db, 1))
    a_s, q_s, kf_s, vf_s, gv_s = _in_proj(xs, *proj_params, ws_s, bs_s, tm=db * dec, cs=db * dec)
    b_s = _sample_attention(page_table, rel_bias, q_s.reshape(db, dec, D_QK),
                            kf_s.reshape(db, dec * H_B, LANES), vf_s.reshape(db, dec * H_B, LANES),
                            cache_k.reshape(-1, PAGE * H_B, DV), cache_v.reshape(-1, PAGE * H_B, DV),
                            lam, row(subln_g), pages=32)
    y_s = _tail(xs, a_s, b_s.reshape(db * dec, D_B), *tail_params, tm=db * dec)

    return (y_p.reshape(nb, seq, d), y_s.reshape(db, dec, d),
            kf_p.reshape(nb, seq, H_B, 2 * DH), vf_p.reshape(nb, seq, H_B, DV),
            kf_s.reshape(db, dec, H_B, 2 * DH), vf_s.reshape(db, dec, H_B, DV),
            gv_s.reshape(db, dec, D_A))
```

```python
import functools
import math

import numpy as np
import jax
import jax.numpy as jnp
from jax import lax
from jax.experimental import pallas as pl
from jax.experimental.pallas import tpu as pltpu

F32 = jnp.float32
BF16 = jnp.bfloat16

EPS = 1e-6
LANES = 128
G_A = 4
CH_A = 128
CHUNK = 128
H_B = 4
DV = 128
DH = 64
D_A = G_A * CH_A
D_QK = H_B * 2 * DH
D_B = H_B * DV
SCALE = DH ** -0.5
LOG2E = math.log2(math.e)
LAMBDA_INIT = 0.8 - 0.6 * math.exp(-0.3 * 0)
NUM_BUCKETS = 32
MAX_EXACT = NUM_BUCKETS // 2
MAX_DISTANCE = 128
PAGE = 128
ONES = 16
NEG = -1e30
VMEM_LIMIT = 56 * 1024 * 1024


def _bucket_thresholds():
    n = np.arange(0, 2 * MAX_DISTANCE)
    nf = np.maximum(n, 1).astype(np.float64)
    large = MAX_EXACT + (np.log(nf / MAX_EXACT) / math.log(MAX_DISTANCE / MAX_EXACT)
                         * (NUM_BUCKETS - MAX_EXACT)).astype(np.int32)
    bucket = np.where(n < MAX_EXACT, n, np.minimum(large, NUM_BUCKETS - 1))
    assert np.all(np.diff(bucket) >= 0) and bucket[MAX_DISTANCE - 1] == NUM_BUCKETS - 1
    return [int(n[bucket >= b].min()) for b in range(NUM_BUCKETS)]


BUCKET_START = _bucket_thresholds()


def _rms(x, g):
    ms = jnp.mean(x * x, axis=-1, keepdims=True)
    return x * lax.rsqrt(ms + EPS) * g


def _dot(a, b):
    return jnp.dot(a, b, preferred_element_type=F32)


def _dot_nt(a, b):
    return lax.dot_general(a, b, (((1,), (1,)), ((), ())), preferred_element_type=F32)


def _bias_from_distance(n, value_of_bucket):
    val = jnp.zeros(n.shape, F32) + value_of_bucket(0)
    for b in range(1, NUM_BUCKETS):
        val = jnp.where(n >= BUCKET_START[b], value_of_bucket(b), val)
    return val


def _lambda_full(lam_ref):
    s1 = jnp.sum(lam_ref[0:1, :] * lam_ref[1:2, :], axis=-1, keepdims=True)
    s2 = jnp.sum(lam_ref[2:3, :] * lam_ref[3:4, :], axis=-1, keepdims=True)
    return jnp.exp(s1) - jnp.exp(s2) + LAMBDA_INIT


def _inproj_kernel(x_ref, g1_ref, win_ref, lng_ref, lnb_ref, ws_ref, bs_ref, *out_refs, cs, prompt):
    if prompt:
        a_ref, q_ref, kf_ref, vf_ref, kb_ref, vt_ref = out_refs
    else:
        a_ref, q_ref, kf_ref, vf_ref, vn_ref = out_refs
    tm = x_ref.shape[0]
    xn = _rms(x_ref[...], g1_ref[...]).astype(BF16)

    v = _dot(xn, win_ref[:, D_A:2 * D_A])
    vn16 = []
    for g in range(G_A):
        gs = slice(g * CH_A, (g + 1) * CH_A)
        vg = v[:, gs]
        mu = jnp.mean(vg, axis=-1, keepdims=True)
        d = vg - mu
        var = jnp.mean(d * d, axis=-1, keepdims=True)
        vn = d * lax.rsqrt(var + EPS) * lng_ref[:, gs] + lnb_ref[:, gs]
        if not prompt:
            vn_ref[:, gs] = vn
        vn16.append(vn.astype(BF16))

    vb = _dot(xn, win_ref[:, 2 * D_A + 2 * D_QK:])
    for hh in range(H_B):
        vf_ref[pl.ds(hh, tm, stride=H_B), :] = vb[:, hh * LANES:(hh + 1) * LANES]
    if prompt:
        vt_ref[0] = vb.T.astype(BF16)

    q = _dot(xn, win_ref[:, 2 * D_A:2 * D_A + D_QK])
    if prompt:
        q_ref[0] = (q * (SCALE * LOG2E)).T.astype(BF16)
    else:
        q_ref[...] = (q * SCALE).astype(BF16)

    u = _dot(xn, win_ref[:, 0:D_A])
    row = lax.broadcasted_iota(jnp.int32, (cs, cs), 0)
    col = lax.broadcasted_iota(jnp.int32, (cs, cs), 1)
    for g in range(G_A):
        gs = slice(g * CH_A, (g + 1) * CH_A)
        w = jnp.where(row >= col, ws_ref[g], 0.0).astype(BF16)
        for c in range(tm // cs):
            rs = slice(c * cs, (c + 1) * cs)
            s = _dot(w, vn16[g][rs, :]) + bs_ref[:, gs]
            a_ref[rs, gs] = (u[rs, gs] * s).astype(BF16)

    k = _dot(xn, win_ref[:, 2 * D_A + D_QK:2 * D_A + 2 * D_QK])
    for hh in range(H_B):
        kf_ref[pl.ds(hh, tm, stride=H_B), :] = k[:, hh * LANES:(hh + 1) * LANES]
    if prompt:
        kb_ref[...] = k.astype(BF16)


def _in_proj(x, g1, win16, lng, lnb, ws, bs_full, *, tm, cs, seq=None):
    t, d = x.shape
    d_in = win16.shape[1]
    const = lambda i: (0, 0)
    tile = lambda w: pl.BlockSpec((tm, w), lambda i: (i, 0))
    heads = pl.BlockSpec((tm * H_B, LANES), lambda i: (i, 0))
    if seq is not None:
        tiles = seq // tm
        transposed = pl.BlockSpec((1, D_QK, tm), lambda i: (i // tiles, 0, i % tiles))
        transposed_shape = jax.ShapeDtypeStruct((t // seq, D_QK, seq), BF16)
    out_shape = [jax.ShapeDtypeStruct((t, D_A), BF16),
                 jax.ShapeDtypeStruct((t, D_QK), BF16) if seq is None else transposed_shape,
                 jax.ShapeDtypeStruct((t * H_B, LANES), F32), jax.ShapeDtypeStruct((t * H_B, LANES), F32)]
    out_specs = [tile(D_A), tile(D_QK) if seq is None else transposed, heads, heads]
    if seq is not None:
        out_shape += [jax.ShapeDtypeStruct((t, D_QK), BF16), transposed_shape]
        out_specs += [tile(D_QK), transposed]
    else:
        out_shape.append(jax.ShapeDtypeStruct((t, D_A), F32))
        out_specs.append(tile(D_A))
    return pl.pallas_call(
        functools.partial(_inproj_kernel, cs=cs, prompt=seq is not None),
        grid=(t // tm,),
        in_specs=[tile(d), pl.BlockSpec((1, d), const), pl.BlockSpec((d, d_in), const),
                  pl.BlockSpec((1, D_A), const), pl.BlockSpec((1, D_A), const),
                  pl.BlockSpec((G_A, cs, cs), lambda i: (0, 0, 0)), pl.BlockSpec((cs, D_A), const)],
        out_specs=out_specs,
        out_shape=out_shape,
        compiler_params=pltpu.CompilerParams(dimension_semantics=("arbitrary",),
                                             vmem_limit_bytes=VMEM_LIMIT),
        name="in_proj",
    )(x, g1, win16, lng, lnb, ws, bs_full)


def _attn_kernel(rb_ref, q_ref, k_ref, vt_ref, lam_ref, sg_ref, o_ref,
                 diag_scr, corner_scr, qzt_scr, vt1_scr, sa_scr, sb_scr, sc_scr, mxa_scr, mxb_scr, mxc_scr,
                 m_scr, acc_scr, *, t):
    b, hg = pl.program_id(0), pl.program_id(1)
    hp = vt1_scr.shape[0]
    nq = k_ref.shape[1] // t
    c = LANES

    @pl.when((b == 0) & (hg == 0))
    def _build_bias_tiles():
        d0 = lax.broadcasted_iota(jnp.int32, (t, t), 1) - lax.broadcasted_iota(jnp.int32, (t, t), 0)
        dc = c + lax.broadcasted_iota(jnp.int32, (c, c), 1) - lax.broadcasted_iota(jnp.int32, (c, c), 0)
        for hh in range(H_B):
            near = _bias_from_distance(jnp.maximum(d0, 0), lambda bk: rb_ref[bk, hh]) * LOG2E
            diag_scr[hh] = jnp.where(d0 >= 0, near, NEG)
            corner_scr[hh] = (_bias_from_distance(dc, lambda bk: rb_ref[bk, hh])
                              - rb_ref[NUM_BUCKETS - 1, hh]) * LOG2E

    for e in range(hp):
        vt1_scr[e, 0:DV, :] = vt_ref[0, e * DV:(e + 1) * DV, :]
        vt1_scr[e, DV:DV + ONES, :] = jnp.ones((ONES, vt1_scr.shape[2]), BF16)

    def stage_queries(slot, tile):
        depth = lax.broadcasted_iota(jnp.int32, (LANES, t), 0)
        for e in range(hp):
            qt = q_ref[0, e * LANES:(e + 1) * LANES, pl.ds(pl.multiple_of(tile * t, t), t)]
            zero = jnp.zeros_like(qt)
            qzt_scr[slot, e] = jnp.concatenate([jnp.where(depth < DH, qt, zero),
                                                jnp.where(depth >= DH, qt, zero)], axis=1)

    def reset_state():
        for e in range(hp):
            m_scr[e] = jnp.full(m_scr.shape[1:], -jnp.inf, F32)
            acc_scr[e] = jnp.zeros(acc_scr.shape[1:], F32)

    def issue(buf, slot, kj):
        s_scr, mx_scr = buf
        for e in range(hp):
            k = k_ref[0, pl.ds(pl.multiple_of(kj * t, t), t), e * LANES:(e + 1) * LANES]
            s = _dot(k, qzt_scr[slot, e])
            s_scr[e] = s
            mx_scr[e] = jnp.max(s, axis=0, keepdims=True)

    def accumulate(e, s, mx, shift, kj):
        m_old = m_scr[e]
        m_new = jnp.maximum(m_old, mx + shift)
        alpha = jnp.exp2(m_old - m_new)
        p = jnp.exp2((s - (m_new - shift)).astype(BF16))
        vt1 = vt1_scr[e, :, pl.ds(pl.multiple_of(kj * t, t), t)]
        acc_scr[e] = alpha * acc_scr[e] + _dot(vt1, p)
        m_scr[e] = m_new

    def consume(buf, kj, qi):
        s_scr, mx_scr = buf
        near = (kj == qi - 1).astype(F32)
        for e in range(hp):
            hh = hg * hp + e
            fix = near * corner_scr[hh]
            c0 = s_scr[e, t - c:t, 0:c] + fix
            c1 = s_scr[e, t - c:t, t:t + c] + fix
            s_scr[e, t - c:t, 0:c] = c0
            s_scr[e, t - c:t, t:t + c] = c1
            mx = mx_scr[e]
            mx = jnp.concatenate([jnp.maximum(mx[:, 0:c], jnp.max(c0, axis=0, keepdims=True)), mx[:, c:t],
                                  jnp.maximum(mx[:, t:t + c], jnp.max(c1, axis=0, keepdims=True)),
                                  mx[:, t + c:]], axis=1)
            accumulate(e, s_scr[e], mx, rb_ref[NUM_BUCKETS - 1, hh] * LOG2E, kj)

    def finish_tile(buf, qi):
        lam = _lambda_full(lam_ref)
        for e in range(hp):
            bias = diag_scr[hg * hp + e]
            s = buf[0][e]
            s = jnp.concatenate([s[:, 0:t] + bias, s[:, t:2 * t] + bias], axis=1)
            accumulate(e, s, jnp.max(s, axis=0, keepdims=True), 0.0, qi)
        for e in range(hp):
            ot = acc_scr[e, 0:DV, :] / acc_scr[e, DV:DV + 1, :]
            ot = ot[:, 0:t] - lam * ot[:, t:2 * t]
            ms = jnp.mean(ot * ot, axis=0, keepdims=True)
            o_ref[0, e * DV:(e + 1) * DV, pl.ds(pl.multiple_of(qi * t, t), t)] = (
                ot * lax.rsqrt(ms + EPS) * sg_ref[...] * (1.0 - LAMBDA_INIT)).astype(o_ref.dtype)
        reset_state()

    buf_a, buf_b, buf_c = (sa_scr, mxa_scr), (sb_scr, mxb_scr), (sc_scr, mxc_scr)

    def prefetch_next_tile(qi):
        stage_queries(1, jnp.minimum(qi + 1, nq - 1))
        issue(buf_c, 1, 0)

    def tile(qi, carry):
        @pl.when(qi == 0)
        def _first_tile():
            finish_tile(buf_c, qi)
            stage_queries(0, 1)
            issue(buf_c, 0, 0)

        @pl.when(qi > 0)
        def _first_block():
            issue(buf_a, 0, 1)
            consume(buf_c, 0, qi)

        def block_pair(i, inner):
            issue(buf_b, 0, 2 * i + 2)
            consume(buf_a, 2 * i + 1, qi)
            issue(buf_a, 0, 2 * i + 3)
            consume(buf_b, 2 * i + 2, qi)
            return inner

        lax.fori_loop(0, lax.shift_right_arithmetic(qi - 1, 1), block_pair, 0)

        @pl.when((qi > 0) & (qi % 2 == 1))
        def _diagonal_in_a():
            prefetch_next_tile(qi)
            finish_tile(buf_a, qi)
            qzt_scr[0] = qzt_scr[1]

        @pl.when((qi > 0) & (qi % 2 == 0))
        def _diagonal_in_b():
            issue(buf_b, 0, qi)
            consume(buf_a, qi - 1, qi)
            prefetch_next_tile(qi)
            finish_tile(buf_b, qi)
            qzt_scr[0] = qzt_scr[1]

        return carry

    stage_queries(0, 0)
    reset_state()
    issue(buf_c, 0, 0)
    lax.fori_loop(0, nq, tile, 0)


def _prompt_attention(rel_bias, qt, k, vt, lam, sg_col, *, t, hp):
    b, s, _ = k.shape
    const = lambda bi, hi: (0, 0)
    w = hp * LANES
    transposed = pl.BlockSpec((1, w, s), lambda bi, hi: (bi, hi, 0))
    scores = pltpu.VMEM((hp, t, 2 * t), F32)
    per_query = pltpu.VMEM((hp, 1, 2 * t), F32)
    return pl.pallas_call(
        functools.partial(_attn_kernel, t=t),
        grid=(b, H_B // hp),
        in_specs=[pl.BlockSpec(memory_space=pltpu.SMEM), transposed,
                  pl.BlockSpec((1, s, w), lambda bi, hi: (bi, 0, hi)), transposed,
                  pl.BlockSpec((4, DH), const), pl.BlockSpec((DV, 1), const)],
        out_specs=transposed,
        out_shape=jax.ShapeDtypeStruct((b, D_B, s), BF16),
        scratch_shapes=[pltpu.VMEM((H_B, t, t), F32), pltpu.VMEM((H_B, LANES, LANES), F32),
                        pltpu.VMEM((2, hp, LANES, 2 * t), BF16), pltpu.VMEM((hp, DV + ONES, s), BF16),
                        scores, scores, scores, per_query, per_query, per_query,
                        per_query, pltpu.VMEM((hp, DV + ONES, 2 * t), F32)],
        compiler_params=pltpu.CompilerParams(dimension_semantics=("arbitrary",) * 2,
                                             vmem_limit_bytes=VMEM_LIMIT),
        name="prompt_attn",
    )(rel_bias, qt, k, vt, lam, sg_col)


def _paged_kernel(pt_ref, rb_ref, q_ref, kn_ref, vn_ref, lam_ref, sg_ref, *rest, pages, dec):
    k_refs, v_refs = rest[:pages], rest[pages:2 * pages]
    o_ref, qh_scr, new_scr, tile_scr, far_scr, m_scr, l_scr, acc_scr = rest[2 * pages:]
    del pt_ref
    b, st = pl.program_id(0), pl.program_id(1)
    n_st = pl.num_programs(1)
    hrows = 2 * dec
    rows = H_B * hrows

    @pl.when((b == 0) & (st == 0))
    def _build_bias_tiles():
        r = lax.broadcasted_iota(jnp.int32, (rows, PAGE), 0)
        j = lax.broadcasted_iota(jnp.int32, (rows, PAGE), 1)
        head, i = r // hrows, r % dec

        def row_value(bk):
            val = jnp.zeros((rows, PAGE), F32) + rb_ref[bk, H_B - 1]
            for hh in range(H_B - 2, -1, -1):
                val = jnp.where(head == hh, rb_ref[bk, hh], val)
            return val

        tile_scr[0] = _bias_from_distance(PAGE + i - j, row_value)
        tile_scr[1] = jnp.where(j <= i, _bias_from_distance(jnp.maximum(i - j, 0), row_value), NEG)
        far_scr[...] = row_value(NUM_BUCKETS - 1)[:, 0:1]
        new_scr[...] = jnp.zeros_like(new_scr)

    @pl.when(st == 0)
    def _start_batch():
        r = lax.broadcasted_iota(jnp.int32, (hrows, LANES), 0)
        lane = lax.broadcasted_iota(jnp.int32, (hrows, LANES), 1)
        for hh in range(H_B):
            qh = q_ref[0, :, hh * LANES:(hh + 1) * LANES].astype(F32)
            qh = jnp.concatenate([qh, qh], axis=0)
            qh_scr[hh] = jnp.where(lane // DH == r // dec, qh, 0.0).astype(BF16)
        m_scr[...] = jnp.full_like(m_scr, -jnp.inf)
        l_scr[...] = jnp.zeros_like(l_scr)
        acc_scr[...] = jnp.zeros_like(acc_scr)

    def scores(head_keys):
        return jnp.concatenate([_dot_nt(qh_scr[hh], head_keys(hh)) for hh in range(H_B)], axis=0)

    def update(s, shift, head_values):
        m_old = m_scr[...]
        m_new = jnp.maximum(m_old, jnp.max(s, axis=-1, keepdims=True) + shift)
        alpha = jnp.exp(m_old - m_new)
        p = jnp.exp(s - (m_new - shift))
        l_scr[...] = alpha * l_scr[...] + jnp.sum(p, axis=-1, keepdims=True)
        p = p.astype(BF16)
        pv = jnp.concatenate([_dot(p[hh * hrows:(hh + 1) * hrows], head_values(hh))
                              for hh in range(H_B)], axis=0)
        acc_scr[...] = alpha * acc_scr[...] + pv
        m_scr[...] = m_new

    def cached(refs, first, last):
        return lambda hh: jnp.concatenate(
            [refs[p][0, pl.ds(hh, PAGE, stride=H_B), :] for p in range(first, last)], axis=0).astype(BF16)

    def fresh(which):
        return lambda hh: new_scr[which, :, hh * LANES:(hh + 1) * LANES].astype(BF16)

    @pl.when(st < n_st - 1)
    def _far_pages():
        update(scores(cached(k_refs, 0, pages)), far_scr[...], cached(v_refs, 0, pages))

    @pl.when(st == n_st - 1)
    def _last_pages_and_new_rows():
        if pages > 1:
            update(scores(cached(k_refs, 0, pages - 1)), far_scr[...], cached(v_refs, 0, pages - 1))
        update(scores(cached(k_refs, pages - 1, pages)) + tile_scr[0], 0.0,
               cached(v_refs, pages - 1, pages))
        for hh in range(H_B):
            new_scr[0, 0:dec, hh * LANES:(hh + 1) * LANES] = kn_ref[0, pl.ds(hh, dec, stride=H_B), :]
            new_scr[1, 0:dec, hh * LANES:(hh + 1) * LANES] = vn_ref[0, pl.ds(hh, dec, stride=H_B), :]
        update(scores(fresh(0)) + tile_scr[1], 0.0, fresh(1))

        o = acc_scr[...] / l_scr[...]
        lam = _lambda_full(lam_ref)
        for hh in range(H_B):
            r0 = hh * hrows
            oh = o[r0:r0 + dec, :] - lam * o[r0 + dec:r0 + hrows, :]
            o_ref[0, :, hh * DV:(hh + 1) * DV] = (_rms(oh, sg_ref[...])
                                                  * (1.0 - LAMBDA_INIT)).astype(o_ref.dtype)


def _sample_attention(page_table, rel_bias, q, k_new, v_new, cache_k, cache_v, lam, sg, *, pages):
    nb, dec, _ = q.shape
    n_pages = page_table.shape[1]
    n_st = n_pages // pages
    rows = 2 * H_B * dec
    const2 = lambda bi, si, pt: (0, 0)
    per_batch = pl.BlockSpec((1, dec, D_QK), lambda bi, si, pt: (bi, 0, 0))
    new_rows = pl.BlockSpec((1, dec * H_B, LANES), lambda bi, si, pt: (bi, 0, 0))

    def page_spec(p):
        return pl.BlockSpec((1, PAGE * H_B, LANES),
                            lambda bi, si, pt: (pt[bi * n_pages + si * pages + p], 0, 0))

    grid_spec = pltpu.PrefetchScalarGridSpec(
        num_scalar_prefetch=1,
        grid=(nb, n_st),
        in_specs=[pl.BlockSpec(memory_space=pltpu.SMEM), per_batch, new_rows, new_rows,
                  pl.BlockSpec((4, DH), const2), pl.BlockSpec((1, DV), const2)]
                 + [page_spec(p) for p in range(pages)] * 2,
        out_specs=per_batch,
        scratch_shapes=[pltpu.VMEM((H_B, 2 * dec, LANES), BF16), pltpu.VMEM((2, PAGE, D_QK), F32),
                        pltpu.VMEM((2, rows, PAGE), F32), pltpu.VMEM((rows, 1), F32),
                        pltpu.VMEM((rows, 1), F32), pltpu.VMEM((rows, 1), F32),
                        pltpu.VMEM((rows, DV), F32)],
    )
    return pl.pallas_call(
        functools.partial(_paged_kernel, pages=pages, dec=dec),
        grid_spec=grid_spec,
        out_shape=jax.ShapeDtypeStruct((nb, dec, D_B), BF16),
        compiler_params=pltpu.CompilerParams(dimension_semantics=("arbitrary", "arbitrary"),
                                             vmem_limit_bytes=VMEM_LIMIT),
        name="paged_attn",
    )(page_table.reshape(-1), rel_bias, q, k_new, v_new, lam, sg,
      *([cache_k] * pages), *([cache_v] * pages))


def _tail_kernel(x_ref, a_ref, b_ref, wo_ref, g2_ref, wg_ref, wu_ref, wd_ref, gf_ref, y_ref, *, b_transposed):
    tm = x_ref.shape[0]
    halves = [slice(0, tm // 2), slice(tm // 2, tm)]
    hs = []
    for rs in halves:
        if b_transposed:
            bw = lax.dot_general(b_ref[0, :, rs], wo_ref[D_A:, :], (((0,), (0,)), ((), ())),
                                 preferred_element_type=F32)
        else:
            bw = _dot(b_ref[rs, :], wo_ref[D_A:, :])
        hs.append(x_ref[rs, :] + _dot(a_ref[rs, :], wo_ref[0:D_A, :]) + bw)
    fs = []
    for h in hs:
        hn = _rms(h, g2_ref[...]).astype(BF16)
        gate = _dot(hn, wg_ref[...])
        fs.append((gate * (1.0 / (1.0 + jnp.exp(-gate))) * _dot(hn, wu_ref[...])).astype(BF16))
    for rs, h, f in zip(halves, hs, fs):
        y_ref[rs, :] = _rms(h + _dot(f, wd_ref[...]), gf_ref[...])


def _tail(x, a, bmix, wo16, g2, wg16, wu16, wd16, gf, *, tm):
    t, d = x.shape
    d_ff = wg16.shape[1]
    const = lambda i: (0, 0)
    resident = lambda shape: pl.BlockSpec(shape, const, pipeline_mode=pl.Buffered(1))
    b_transposed = bmix.ndim == 3
    if b_transposed:
        tiles = bmix.shape[2] // tm
        b_spec = pl.BlockSpec((1, D_B, tm), lambda i: (i // tiles, 0, i % tiles))
    else:
        b_spec = pl.BlockSpec((tm, D_B), lambda i: (i, 0))
    return pl.pallas_call(
        functools.partial(_tail_kernel, b_transposed=b_transposed),
        grid=(t // tm,),
        in_specs=[pl.BlockSpec((tm, d), lambda i: (i, 0)), pl.BlockSpec((tm, D_A), lambda i: (i, 0)),
                  b_spec, resident((D_A + D_B, d)),
                  pl.BlockSpec((1, d), const), resident((d, d_ff)), resident((d, d_ff)),
                  resident((d_ff, d)), pl.BlockSpec((1, d), const)],
        out_specs=pl.BlockSpec((tm, d), lambda i: (i, 0)),
        out_shape=jax.ShapeDtypeStruct((t, d), F32),
        compiler_params=pltpu.CompilerParams(dimension_semantics=("arbitrary",),
                                             vmem_limit_bytes=VMEM_LIMIT),
        name="tail",
    )(x, a, bmix, wo16, g2, wg16, wu16, wd16, gf)


def kernel(x_prompt, x_sample, cache_k, cache_v, page_table, norm1_g, w_in, gmlp_ln_g, gmlp_ln_b, gmlp_ws, gmlp_bs, lambda_q1, lambda_k1, lambda_q2, lambda_k2, subln_g, rel_bias, w_out, norm2_g, w_gate, w_up, w_down, final_g):
    nb, seq, d = x_prompt.shape
    db, dec, _ = x_sample.shape
    row = lambda p: p.reshape(1, -1)

    win16, wo16 = w_in.astype(BF16), w_out.astype(BF16)
    wg16, wu16, wd16 = w_gate.astype(BF16), w_up.astype(BF16), w_down.astype(BF16)
    lam = jnp.stack([lambda_q1, lambda_k1, lambda_q2, lambda_k2])
    proj_params = (row(norm1_g), win16, row(gmlp_ln_g), row(gmlp_ln_b))
    tail_params = (wo16, row(norm2_g), wg16, wu16, wd16, row(final_g))

    xp = x_prompt.reshape(nb * seq, d)
    bs_p = jnp.repeat(gmlp_bs.T, CH_A, axis=1)
    a_p, qt_p, kf_p, vf_p, kb_p, vt_p = _in_proj(xp, *proj_params, gmlp_ws, bs_p,
                                                tm=1024, cs=CHUNK, seq=seq)
    bt_p = _prompt_attention(rel_bias, qt_p, kb_p.reshape(nb, seq, D_QK), vt_p, lam,
                             subln_g.reshape(DV, 1), t=512, hp=2)
    y_p = _tail(xp, a_p, bt_p, *tail_params, tm=512)

    xs = x_sample.reshape(db * dec, d)
    pos = np.arange(db * dec)
    pick = jnp.asarray(pos[:, None] % dec == np.arange(dec)[None, :], F32)
    same = jnp.asarray(pos[:, None] // dec == pos[None, :] // dec)
    ws_s = jnp.where(same, jnp.einsum('rt,gts,cs->grc', pick, gmlp_ws[:, :dec, :dec], pick,
                                      precision=lax.Precision.HIGHEST), 0.0)
    bs_s = jnp.tile(jnp.repeat(gmlp_bs[:, :dec].T, CH_A, axis=1), (db, 1))
    a_s, q_s, kf_s, vf_s, gv_s = _in_proj(xs, *proj_params, ws_s, bs_s, tm=db * dec, cs=db * dec)
    b_s = _sample_attention(page_table, rel_bias, q_s.reshape(db, dec, D_QK),
                            kf_s.reshape(db, dec * H_B, LANES), vf_s.reshape(db, dec * H_B, LANES),
                            cache_k.reshape(-1, PAGE * H_B, DV), cache_v.reshape(-1, PAGE * H_B, DV),
                            lam, row(subln_g), pages=32)
    y_s = _tail(xs, a_s, b_s.reshape(db * dec, D_B), *tail_params, tm=db * dec)

    return (y_p.reshape(nb, seq, d), y_s.reshape(db, dec, d),
            kf_p.reshape(nb, seq, H_B, 2 * DH), vf_p.reshape(nb, seq, H_B, DV),
            kf_s.reshape(db, dec, H_B, 2 * DH), vf_s.reshape(db, dec, H_B, DV),
            gv_s.reshape(db, dec, D_A))
```

```python
import functools
import math

import numpy as np
import jax
import jax.numpy as jnp
from jax import lax
from jax.experimental import pallas as pl
from jax.experimental.pallas import tpu as pltpu

F32 = jnp.float32
BF16 = jnp.bfloat16

EPS = 1e-6
LANES = 128
G_A = 4
CH_A = 128
CHUNK = 128
H_B = 4
DV = 128
DH = 64
D_A = G_A * CH_A
D_QK = H_B * 2 * DH
D_B = H_B * DV
SCALE = DH ** -0.5
LOG2E = math.log2(math.e)
LAYER_INDEX = 1
LAMBDA_INIT = 0.8 - 0.6 * math.exp(-0.3 * (LAYER_INDEX - 1))
NUM_BUCKETS = 32
MAX_EXACT = NUM_BUCKETS // 2
MAX_DISTANCE = 128
PAGE = 128
TAIL_PARTS = 2
ONES = 16
NEG = -1e30
VMEM_LIMIT = 56 * 1024 * 1024


def _bucket_thresholds():
    n = np.arange(0, 2 * MAX_DISTANCE)
    nf = np.maximum(n, 1).astype(np.float64)
    large = MAX_EXACT + (np.log(nf / MAX_EXACT) / math.log(MAX_DISTANCE / MAX_EXACT)
                         * (NUM_BUCKETS - MAX_EXACT)).astype(np.int32)
    bucket = np.where(n < MAX_EXACT, n, np.minimum(large, NUM_BUCKETS - 1))
    assert np.all(np.diff(bucket) >= 0) and bucket[MAX_DISTANCE - 1] == NUM_BUCKETS - 1
    return [int(n[bucket >= b].min()) for b in range(NUM_BUCKETS)]


BUCKET_START = _bucket_thresholds()
FAR = BUCKET_START[-1]
assert FAR <= LANES and FAR <= PAGE


def _rms(x, g):
    ms = jnp.mean(x * x, axis=-1, keepdims=True)
    return x * lax.rsqrt(ms + EPS) * g


def _dot(a, b):
    return jnp.dot(a, b, preferred_element_type=F32)


def _dot_nt(a, b):
    return lax.dot_general(a, b, (((1,), (1,)), ((), ())), preferred_element_type=F32)


def _bias_from_distance(n, value_of_bucket):
    val = jnp.zeros(n.shape, F32) + value_of_bucket(0)
    for b in range(1, NUM_BUCKETS):
        val = jnp.where(n >= BUCKET_START[b], value_of_bucket(b), val)
    return val


def _lambda_full(lam_ref):
    s1 = jnp.sum(lam_ref[0:1, :] * lam_ref[1:2, :], axis=-1, keepdims=True)
    s2 = jnp.sum(lam_ref[2:3, :] * lam_ref[3:4, :], axis=-1, keepdims=True)
    return jnp.exp(s1) - jnp.exp(s2) + LAMBDA_INIT


def _inproj_kernel(x_ref, g1_ref, win_ref, lng_ref, lnb_ref, ws_ref, bs_ref, *out_refs, cs, prompt):
    if prompt:
        a_ref, q_ref, kf_ref, vf_ref, kb_ref, vt_ref = out_refs
    else:
        a_ref, q_ref, kf_ref, vf_ref, vn_ref = out_refs
    tm = x_ref.shape[0]
    xn0 = _rms(x_ref[0:tm // 2, :], g1_ref[...]).astype(BF16)
    v0 = _dot(xn0, win_ref[:, D_A:2 * D_A])
    xn1 = _rms(x_ref[tm // 2:tm, :], g1_ref[...]).astype(BF16)
    v = jnp.concatenate([v0, _dot(xn1, win_ref[:, D_A:2 * D_A])], axis=0)
    xn = jnp.concatenate([xn0, xn1], axis=0)
    vn16 = []
    for g in range(G_A):
        gs = slice(g * CH_A, (g + 1) * CH_A)
        vg = v[:, gs]
        mu = jnp.mean(vg, axis=-1, keepdims=True)
        d = vg - mu
        var = jnp.mean(d * d, axis=-1, keepdims=True)
        vn = d * lax.rsqrt(var + EPS) * lng_ref[:, gs] + lnb_ref[:, gs]
        if not prompt:
            vn_ref[:, gs] = vn
        vn16.append(vn.astype(BF16))

    vb = _dot(xn, win_ref[:, 2 * D_A + 2 * D_QK:])
    for hh in range(H_B):
        vf_ref[pl.ds(hh, tm, stride=H_B), :] = vb[:, hh * LANES:(hh + 1) * LANES]
    if prompt:
        vt_ref[0] = vb.T.astype(BF16)

    q = _dot(xn, win_ref[:, 2 * D_A:2 * D_A + D_QK])
    if prompt:
        q_ref[0] = (q * (SCALE * LOG2E)).T.astype(BF16)
    else:
        q_ref[...] = (q * SCALE).astype(BF16)

    u = _dot(xn, win_ref[:, 0:D_A])
    row = lax.broadcasted_iota(jnp.int32, (cs, cs), 0)
    col = lax.broadcasted_iota(jnp.int32, (cs, cs), 1)
    for g in range(G_A):
        gs = slice(g * CH_A, (g + 1) * CH_A)
        w = jnp.where(row >= col, ws_ref[g], 0.0).astype(BF16)
        for c in range(tm // cs):
            rs = slice(c * cs, (c + 1) * cs)
            s = _dot(w, vn16[g][rs, :]) + bs_ref[:, gs]
            a_ref[rs, gs] = (u[rs, gs] * s).astype(BF16)

    k = _dot(xn, win_ref[:, 2 * D_A + D_QK:2 * D_A + 2 * D_QK])
    for hh in range(H_B):
        kf_ref[pl.ds(hh, tm, stride=H_B), :] = k[:, hh * LANES:(hh + 1) * LANES]
    if prompt:
        kb_ref[...] = k.astype(BF16)


def _in_proj(x, g1, win16, lng, lnb, ws, bs_full, *, tm, cs, seq=None):
    t, d = x.shape
    d_in = win16.shape[1]
    const = lambda i: (0, 0)
    tile = lambda w: pl.BlockSpec((tm, w), lambda i: (i, 0))
    heads = pl.BlockSpec((tm * H_B, LANES), lambda i: (i, 0))
    if seq is not None:
        tiles = seq // tm
        transposed = pl.BlockSpec((1, D_QK, tm), lambda i: (i // tiles, 0, i % tiles))
        transposed_shape = jax.ShapeDtypeStruct((t // seq, D_QK, seq), BF16)
    out_shape = [jax.ShapeDtypeStruct((t, D_A), BF16),
                 jax.ShapeDtypeStruct((t, D_QK), BF16) if seq is None else transposed_shape,
                 jax.ShapeDtypeStruct((t * H_B, LANES), F32), jax.ShapeDtypeStruct((t * H_B, LANES), F32)]
    out_specs = [tile(D_A), tile(D_QK) if seq is None else transposed, heads, heads]
    if seq is not None:
        out_shape += [jax.ShapeDtypeStruct((t, D_QK), BF16), transposed_shape]
        out_specs += [tile(D_QK), transposed]
    else:
        out_shape.append(jax.ShapeDtypeStruct((t, D_A), F32))
        out_specs.append(tile(D_A))
    return pl.pallas_call(
        functools.partial(_inproj_kernel, cs=cs, prompt=seq is not None),
        grid=(t // tm,),
        in_specs=[tile(d), pl.BlockSpec((1, d), const), pl.BlockSpec((d, d_in), const),
                  pl.BlockSpec((1, D_A), const), pl.BlockSpec((1, D_A), const),
                  pl.BlockSpec((G_A, cs, cs), lambda i: (0, 0, 0)), pl.BlockSpec((cs, D_A), const)],
        out_specs=out_specs,
        out_shape=out_shape,
        compiler_params=pltpu.CompilerParams(dimension_semantics=("arbitrary",),
                                             vmem_limit_bytes=VMEM_LIMIT),
        name="in_proj",
    )(x, g1, win16, lng, lnb, ws, bs_full)


def _attn_kernel(rb_ref, q_ref, k_ref, vt_ref, lam_ref, sg_ref, o_ref,
                 diag_scr, corner_scr, qzt_scr, vt1_scr, sa_scr, sb_scr, sc_scr, mxa_scr, mxb_scr, mxc_scr,
                 m_scr, acc_scr, *, t):
    b, hg = pl.program_id(0), pl.program_id(1)
    hp = vt1_scr.shape[0]
    nq = k_ref.shape[1] // t
    c = LANES

    @pl.when((b == 0) & (hg == 0))
    def _build_bias_tiles():
        d0 = lax.broadcasted_iota(jnp.int32, (t, t), 1) - lax.broadcasted_iota(jnp.int32, (t, t), 0)
        dc = c + lax.broadcasted_iota(jnp.int32, (c, c), 1) - lax.broadcasted_iota(jnp.int32, (c, c), 0)
        for hh in range(H_B):
            near = _bias_from_distance(jnp.maximum(d0, 0), lambda bk: rb_ref[bk, hh]) * LOG2E
            diag_scr[hh] = jnp.where(d0 >= 0, near, NEG)
            corner_scr[hh] = (_bias_from_distance(dc, lambda bk: rb_ref[bk, hh])
                              - rb_ref[NUM_BUCKETS - 1, hh]) * LOG2E

    def stage_values():
        for e in range(hp):
            vt1_scr[e, 0:DV, :] = vt_ref[0, e * DV:(e + 1) * DV, :]
            vt1_scr[e, DV:DV + ONES, :] = jnp.ones((ONES, vt1_scr.shape[2]), BF16)

    def stage_queries(slot, tile):
        depth = lax.broadcasted_iota(jnp.int32, (LANES, t), 0)
        for e in range(hp):
            qt = q_ref[0, e * LANES:(e + 1) * LANES, pl.ds(pl.multiple_of(tile * t, t), t)]
            zero = jnp.zeros_like(qt)
            qzt_scr[slot, e] = jnp.concatenate([jnp.where(depth < DH, qt, zero),
                                                jnp.where(depth >= DH, qt, zero)], axis=1)

    def reset_state():
        for e in range(hp):
            m_scr[e] = jnp.full(m_scr.shape[1:], -jnp.inf, F32)
            acc_scr[e] = jnp.zeros(acc_scr.shape[1:], F32)

    def issue(buf, slot, kj):
        s_scr, mx_scr = buf
        for e in range(hp):
            k = k_ref[0, pl.ds(pl.multiple_of(kj * t, t), t), e * LANES:(e + 1) * LANES]
            s = _dot(k, qzt_scr[slot, e])
            s_scr[e] = s
            mx_scr[e] = jnp.max(s, axis=0, keepdims=True)

    def accumulate(e, s, mx, shift, kj):
        m_old = m_scr[e]
        m_new = jnp.maximum(m_old, mx + shift)
        alpha = jnp.exp2(m_old - m_new)
        p = jnp.exp2((s - (m_new - shift)).astype(BF16))
        vt1 = vt1_scr[e, :, pl.ds(pl.multiple_of(kj * t, t), t)]
        acc_scr[e] = alpha * acc_scr[e] + _dot(vt1, p)
        m_scr[e] = m_new

    def consume(buf, kj, qi):
        s_scr, mx_scr = buf
        near = jnp.where(kj == qi - 1, 1.0, 0.0).astype(F32)
        for e in range(hp):
            hh = hg * hp + e
            fix = near * corner_scr[hh]
            c0 = s_scr[e, t - c:t, 0:c] + fix
            c1 = s_scr[e, t - c:t, t:t + c] + fix
            s_scr[e, t - c:t, 0:c] = c0
            s_scr[e, t - c:t, t:t + c] = c1
            mx = mx_scr[e]
            mx = jnp.concatenate([jnp.maximum(mx[:, 0:c], jnp.max(c0, axis=0, keepdims=True)), mx[:, c:t],
                                  jnp.maximum(mx[:, t:t + c], jnp.max(c1, axis=0, keepdims=True)),
                                  mx[:, t + c:]], axis=1)
            accumulate(e, s_scr[e], mx, rb_ref[NUM_BUCKETS - 1, hh] * LOG2E, kj)

    def finish_tile(buf, qi):
        lam = _lambda_full(lam_ref)
        for e in range(hp):
            bias = diag_scr[hg * hp + e]
            s = buf[0][e]
            s = jnp.concatenate([s[:, 0:t] + bias, s[:, t:2 * t] + bias], axis=1)
            accumulate(e, s, jnp.max(s, axis=0, keepdims=True), 0.0, qi)
        for e in range(hp):
            ot = acc_scr[e, 0:DV, :] / acc_scr[e, DV:DV + 1, :]
            ot = ot[:, 0:t] - lam * ot[:, t:2 * t]
            ms = jnp.mean(ot * ot, axis=0, keepdims=True)
            o_ref[0, e * DV:(e + 1) * DV, pl.ds(pl.multiple_of(qi * t, t), t)] = (
                ot * lax.rsqrt(ms + EPS) * sg_ref[...] * (1.0 - LAMBDA_INIT)).astype(o_ref.dtype)
        reset_state()

    buf_a, buf_b, buf_c = (sa_scr, mxa_scr), (sb_scr, mxb_scr), (sc_scr, mxc_scr)

    def prefetch_next_tile(qi):
        stage_queries(1, jnp.minimum(qi + 1, nq - 1))
        issue(buf_c, 1, 0)

    def tile(qi, carry):
        @pl.when(qi == 0)
        def _first_tile():
            finish_tile(buf_c, qi)
            stage_queries(0, 1)
            issue(buf_c, 0, 0)

        @pl.when(qi > 0)
        def _first_block():
            issue(buf_a, 0, 1)
            consume(buf_c, 0, qi)

        def block_pair(i, inner):
            issue(buf_b, 0, 2 * i + 2)
            consume(buf_a, 2 * i + 1, qi)
            issue(buf_a, 0, 2 * i + 3)
            consume(buf_b, 2 * i + 2, qi)
            return inner

        lax.fori_loop(0, lax.shift_right_arithmetic(qi - 1, 1), block_pair, 0)

        @pl.when((qi > 0) & (qi % 2 == 1))
        def _diagonal_in_a():
            prefetch_next_tile(qi)
            finish_tile(buf_a, qi)
            qzt_scr[0] = qzt_scr[1]

        @pl.when((qi > 0) & (qi % 2 == 0))
        def _diagonal_in_b():
            issue(buf_b, 0, qi)
            consume(buf_a, qi - 1, qi)
            prefetch_next_tile(qi)
            finish_tile(buf_b, qi)
            qzt_scr[0] = qzt_scr[1]

        return carry

    stage_queries(0, 0)
    issue(buf_c, 0, 0)
    stage_values()
    reset_state()
    lax.fori_loop(0, nq, tile, 0)


def _prompt_attention(rel_bias, qt, k, vt, lam, sg_col, *, t, hp):
    b, s, _ = k.shape
    assert s % t == 0 and s // t >= 2 and H_B % hp == 0
    const = lambda bi, hi: (0, 0)
    w = hp * LANES
    transposed = pl.BlockSpec((1, w, s), lambda bi, hi: (bi, hi, 0))
    scores = pltpu.VMEM((hp, t, 2 * t), F32)
    per_query = pltpu.VMEM((hp, 1, 2 * t), F32)
    return pl.pallas_call(
        functools.partial(_attn_kernel, t=t),
        grid=(b, H_B // hp),
        in_specs=[pl.BlockSpec(memory_space=pltpu.SMEM), transposed,
                  pl.BlockSpec((1, s, w), lambda bi, hi: (bi, 0, hi)), transposed,
                  pl.BlockSpec((4, DH), const), pl.BlockSpec((DV, 1), const)],
        out_specs=transposed,
        out_shape=jax.ShapeDtypeStruct((b, D_B, s), BF16),
        scratch_shapes=[pltpu.VMEM((H_B, t, t), F32), pltpu.VMEM((H_B, LANES, LANES), F32),
                        pltpu.VMEM((2, hp, LANES, 2 * t), BF16), pltpu.VMEM((hp, DV + ONES, s), BF16),
                        scores, scores, scores, per_query, per_query, per_query,
                        per_query, pltpu.VMEM((hp, DV + ONES, 2 * t), F32)],
        compiler_params=pltpu.CompilerParams(dimension_semantics=("arbitrary",) * 2,
                                             vmem_limit_bytes=VMEM_LIMIT),
        name="prompt_attn",
    )(rel_bias, qt, k, vt, lam, sg_col)


def _paged_kernel(pt_ref, rb_ref, q_ref, kn_ref, vn_ref, lam_ref, sg_ref, *rest, pages, dec):
    k_refs, v_refs = rest[:pages], rest[pages:2 * pages]
    o_ref, qh_scr, new_scr, tile_scr, far_scr, m_scr, l_scr, acc_scr = rest[2 * pages:]
    del pt_ref
    b, st = pl.program_id(0), pl.program_id(1)
    n_st = pl.num_programs(1)
    hrows = 2 * dec
    rows = H_B * hrows

    @pl.when((b == 0) & (st == 0))
    def _build_bias_tiles():
        r = lax.broadcasted_iota(jnp.int32, (rows, PAGE), 0)
        j = lax.broadcasted_iota(jnp.int32, (rows, PAGE), 1)
        head, i = r // hrows, r % dec

        def row_value(bk):
            val = jnp.zeros((rows, PAGE), F32) + rb_ref[bk, H_B - 1]
            for hh in range(H_B - 2, -1, -1):
                val = jnp.where(head == hh, rb_ref[bk, hh], val)
            return val

        tile_scr[0] = _bias_from_distance(PAGE + i - j, row_value)
        tile_scr[1] = jnp.where(j <= i, _bias_from_distance(jnp.maximum(i - j, 0), row_value), NEG)
        far_scr[...] = row_value(NUM_BUCKETS - 1)[:, 0:1]
        new_scr[...] = jnp.zeros_like(new_scr)

    @pl.when(st == 0)
    def _start_batch():
        r = lax.broadcasted_iota(jnp.int32, (hrows, LANES), 0)
        lane = lax.broadcasted_iota(jnp.int32, (hrows, LANES), 1)
        for hh in range(H_B):
            qh = q_ref[0, :, hh * LANES:(hh + 1) * LANES].astype(F32)
            qh = jnp.concatenate([qh, qh], axis=0)
            qh_scr[hh] = jnp.where(lane // DH == r // dec, qh, 0.0).astype(BF16)
        m_scr[...] = jnp.full_like(m_scr, -jnp.inf)
        l_scr[...] = jnp.zeros_like(l_scr)
        acc_scr[...] = jnp.zeros_like(acc_scr)

    def scores(head_keys):
        return jnp.concatenate([_dot_nt(qh_scr[hh], head_keys(hh)) for hh in range(H_B)], axis=0)

    def update(s, shift, head_values):
        m_old = m_scr[...]
        m_new = jnp.maximum(m_old, jnp.max(s, axis=-1, keepdims=True) + shift)
        alpha = jnp.exp(m_old - m_new)
        p = jnp.exp(s - (m_new - shift))
        l_scr[...] = alpha * l_scr[...] + jnp.sum(p, axis=-1, keepdims=True)
        p = p.astype(BF16)
        pv = jnp.concatenate([_dot(p[hh * hrows:(hh + 1) * hrows], head_values(hh))
                              for hh in range(H_B)], axis=0)
        acc_scr[...] = alpha * acc_scr[...] + pv
        m_scr[...] = m_new

    def cached(refs, first, last):
        return lambda hh: jnp.concatenate(
            [refs[p][0, pl.ds(hh, PAGE, stride=H_B), :] for p in range(first, last)], axis=0).astype(BF16)

    def fresh(which):
        return lambda hh: new_scr[which, :, hh * LANES:(hh + 1) * LANES].astype(BF16)

    @pl.when(st < n_st - 1)
    def _far_pages():
        update(scores(cached(k_refs, 0, pages)), far_scr[...], cached(v_refs, 0, pages))

    @pl.when(st == n_st - 1)
    def _last_pages_and_new_rows():
        if pages > 1:
            update(scores(cached(k_refs, 0, pages - 1)), far_scr[...], cached(v_refs, 0, pages - 1))
        update(scores(cached(k_refs, pages - 1, pages)) + tile_scr[0], 0.0,
               cached(v_refs, pages - 1, pages))
        for hh in range(H_B):
            new_scr[0, 0:dec, hh * LANES:(hh + 1) * LANES] = kn_ref[0, pl.ds(hh, dec, stride=H_B), :]
            new_scr[1, 0:dec, hh * LANES:(hh + 1) * LANES] = vn_ref[0, pl.ds(hh, dec, stride=H_B), :]
        update(scores(fresh(0)) + tile_scr[1], 0.0, fresh(1))

        o = acc_scr[...] / l_scr[...]
        lam = _lambda_full(lam_ref)
        for hh in range(H_B):
            r0 = hh * hrows
            oh = o[r0:r0 + dec, :] - lam * o[r0 + dec:r0 + hrows, :]
            o_ref[0, :, hh * DV:(hh + 1) * DV] = (_rms(oh, sg_ref[...])
                                                  * (1.0 - LAMBDA_INIT)).astype(o_ref.dtype)


def _sample_attention(page_table, rel_bias, q, k_new, v_new, cache_k, cache_v, lam, sg, *, pages):
    nb, dec, _ = q.shape
    n_pages = page_table.shape[1]
    n_st = n_pages // pages
    rows = 2 * H_B * dec
    const2 = lambda bi, si, pt: (0, 0)
    per_batch = pl.BlockSpec((1, dec, D_QK), lambda bi, si, pt: (bi, 0, 0))
    new_rows = pl.BlockSpec((1, dec * H_B, LANES), lambda bi, si, pt: (bi, 0, 0))

    def page_spec(p):
        return pl.BlockSpec((1, PAGE * H_B, LANES),
                            lambda bi, si, pt: (pt[bi * n_pages + si * pages + p], 0, 0))

    grid_spec = pltpu.PrefetchScalarGridSpec(
        num_scalar_prefetch=1,
        grid=(nb, n_st),
        in_specs=[pl.BlockSpec(memory_space=pltpu.SMEM), per_batch, new_rows, new_rows,
                  pl.BlockSpec((4, DH), const2), pl.BlockSpec((1, DV), const2)]
                 + [page_spec(p) for p in range(pages)] * 2,
        out_specs=per_batch,
        scratch_shapes=[pltpu.VMEM((H_B, 2 * dec, LANES), BF16), pltpu.VMEM((2, PAGE, D_QK), F32),
                        pltpu.VMEM((2, rows, PAGE), F32), pltpu.VMEM((rows, 1), F32),
                        pltpu.VMEM((rows, 1), F32), pltpu.VMEM((rows, 1), F32),
                        pltpu.VMEM((rows, DV), F32)],
    )
    return pl.pallas_call(
        functools.partial(_paged_kernel, pages=pages, dec=dec),
        grid_spec=grid_spec,
        out_shape=jax.ShapeDtypeStruct((nb, dec, D_B), BF16),
        compiler_params=pltpu.CompilerParams(dimension_semantics=("arbitrary", "arbitrary"),
                                             vmem_limit_bytes=VMEM_LIMIT),
        name="paged_attn",
    )(page_table.reshape(-1), rel_bias, q, k_new, v_new, lam, sg,
      *([cache_k] * pages), *([cache_v] * pages))


def _tail_kernel(x_ref, a_ref, b_ref, wo_ref, g2_ref, wg_ref, wu_ref, wd_ref, gf_ref, y_ref, *, b_transposed):
    tm = x_ref.shape[0]
    halves = [slice(p * tm // TAIL_PARTS, (p + 1) * tm // TAIL_PARTS) for p in range(TAIL_PARTS)]
    hs = []
    for rs in halves:
        if b_transposed:
            bw = lax.dot_general(b_ref[0, :, rs], wo_ref[D_A:, :], (((0,), (0,)), ((), ())),
                                 preferred_element_type=F32)
        else:
            bw = _dot(b_ref[rs, :], wo_ref[D_A:, :])
        hs.append(x_ref[rs, :] + _dot(a_ref[rs, :], wo_ref[0:D_A, :]) + bw)
    fs = []
    for h in hs:
        hn = _rms(h, g2_ref[...]).astype(BF16)
        gate = _dot(hn, wg_ref[...])
        fs.append((gate * (1.0 / (1.0 + jnp.exp(-gate))) * _dot(hn, wu_ref[...])).astype(BF16))
    for rs, h, f in zip(halves, hs, fs):
        y_ref[rs, :] = _rms(h + _dot(f, wd_ref[...]), gf_ref[...])


def _tail(x, a, bmix, wo16, g2, wg16, wu16, wd16, gf, *, tm):
    t, d = x.shape
    d_ff = wg16.shape[1]
    const = lambda i: (0, 0)
    resident = lambda shape: pl.BlockSpec(shape, const, pipeline_mode=pl.Buffered(1))
    b_transposed = bmix.ndim == 3
    if b_transposed:
        tiles = bmix.shape[2] // tm
        b_spec = pl.BlockSpec((1, D_B, tm), lambda i: (i // tiles, 0, i % tiles))
    else:
        b_spec = pl.BlockSpec((tm, D_B), lambda i: (i, 0))
    return pl.pallas_call(
        functools.partial(_tail_kernel, b_transposed=b_transposed),
        grid=(t // tm,),
        in_specs=[pl.BlockSpec((tm, d), lambda i: (i, 0)), pl.BlockSpec((tm, D_A), lambda i: (i, 0)),
                  b_spec, resident((D_A + D_B, d)),
                  pl.BlockSpec((1, d), const), resident((d, d_ff)), resident((d, d_ff)),
                  resident((d_ff, d)), pl.BlockSpec((1, d), const)],
        out_specs=pl.BlockSpec((tm, d), lambda i: (i, 0)),
        out_shape=jax.ShapeDtypeStruct((t, d), F32),
        compiler_params=pltpu.CompilerParams(dimension_semantics=("arbitrary",),
                                             vmem_limit_bytes=VMEM_LIMIT),
        name="tail",
    )(x, a, bmix, wo16, g2, wg16, wu16, wd16, gf)


def kernel(x_prompt, x_sample, cache_k, cache_v, page_table, norm1_g, w_in, gmlp_ln_g, gmlp_ln_b, gmlp_ws, gmlp_bs, lambda_q1, lambda_k1, lambda_q2, lambda_k2, subln_g, rel_bias, w_out, norm2_g, w_gate, w_up, w_down, final_g):
    nb, seq, d = x_prompt.shape
    db, dec, _ = x_sample.shape
    row = lambda p: p.reshape(1, -1)

    win16, wo16 = w_in.astype(BF16), w_out.astype(BF16)
    wg16, wu16, wd16 = w_gate.astype(BF16), w_up.astype(BF16), w_down.astype(BF16)
    lam = jnp.stack([lambda_q1, lambda_k1, lambda_q2, lambda_k2])
    proj_params = (row(norm1_g), win16, row(gmlp_ln_g), row(gmlp_ln_b))
    tail_params = (wo16, row(norm2_g), wg16, wu16, wd16, row(final_g))

    xp = x_prompt.reshape(nb * seq, d)
    bs_p = jnp.repeat(gmlp_bs.T, CH_A, axis=1)
    a_p, qt_p, kf_p, vf_p, kb_p, vt_p = _in_proj(xp, *proj_params, gmlp_ws, bs_p,
                                                tm=1024, cs=CHUNK, seq=seq)
    bt_p = _prompt_attention(rel_bias, qt_p, kb_p.reshape(nb, seq, D_QK), vt_p, lam,
                             subln_g.reshape(DV, 1), t=512, hp=2)
    y_p = _tail(xp, a_p, bt_p, *tail_params, tm=512)

    xs = x_sample.reshape(db * dec, d)
    pos = np.arange(db * dec)
    pick = jnp.asarray(pos[:, None] % dec == np.arange(dec)[None, :], F32)
    same = jnp.asarray(pos[:, None] // dec == pos[None, :] // dec)
    ws_s = jnp.where(same, jnp.einsum('rt,gts,cs->grc', pick, gmlp_ws[:, :dec, :dec], pick,
                                      precision=lax.Precision.HIGHEST), 0.0)
    bs_s = jnp.tile(jnp.repeat(gmlp_bs[:, :dec].T, CH_A, axis=1), (db, 1))
    a_s, q_s, kf_s, vf_s, gv_s = _in_proj(xs, *proj_params, ws_s, bs_s, tm=db * dec, cs=db * dec)
    b_s = _sample_attention(page_table, rel_bias, q_s.reshape(db, dec, D_QK),
                            kf_s.reshape(db, dec * H_B, LANES), vf_s.reshape(db, dec * H_B, LANES),
                            cache_k.reshape(-1, PAGE * H_B, DV), cache_v.reshape(-1, PAGE * H_B, DV),
                            lam, row(subln_g), pages=32)
    y_s = _tail(xs, a_s, b_s.reshape(db * dec, D_B), *tail_params, tm=db * dec)

    return (y_p.reshape(nb, seq, d), y_s.reshape(db, dec, d),
            kf_p.reshape(nb, seq, H_B, 2 * DH), vf_p.reshape(nb, seq, H_B, DV),
            kf_s.reshape(db, dec, H_B, 2 * DH), vf_s.reshape(db, dec, H_B, DV),
            gv_s.reshape(db, dec, D_A))
```

```python
import functools
import math

import numpy as np
import jax
import jax.numpy as jnp
from jax import lax
from jax.experimental import pallas as pl
from jax.experimental.pallas import tpu as pltpu

F32 = jnp.float32
BF16 = jnp.bfloat16

EPS = 1e-6
LANES = 128
G_A = 4
CH_A = 128
CHUNK = 128
H_B = 4
DV = 128
DH = 64
D_A = G_A * CH_A
D_QK = H_B * 2 * DH
D_B = H_B * DV
SCALE = DH ** -0.5
LOG2E = math.log2(math.e)
LAYER_INDEX = 1
LAMBDA_INIT = 0.8 - 0.6 * math.exp(-0.3 * (LAYER_INDEX - 1))
NUM_BUCKETS = 32
MAX_EXACT = NUM_BUCKETS // 2
MAX_DISTANCE = 128
PAGE = 128
TAIL_PARTS = 2
RING = 3
ONES = 16
NEG = -1e30
VMEM_LIMIT = 56 * 1024 * 1024


def _bucket_thresholds():
    n = np.arange(0, 2 * MAX_DISTANCE)
    nf = np.maximum(n, 1).astype(np.float64)
    large = MAX_EXACT + (np.log(nf / MAX_EXACT) / math.log(MAX_DISTANCE / MAX_EXACT)
                         * (NUM_BUCKETS - MAX_EXACT)).astype(np.int32)
    bucket = np.where(n < MAX_EXACT, n, np.minimum(large, NUM_BUCKETS - 1))
    assert np.all(np.diff(bucket) >= 0) and bucket[MAX_DISTANCE - 1] == NUM_BUCKETS - 1
    return [int(n[bucket >= b].min()) for b in range(NUM_BUCKETS)]


BUCKET_START = _bucket_thresholds()
FAR = BUCKET_START[-1]
assert FAR <= LANES and FAR <= PAGE


def _rms(x, g):
    ms = jnp.mean(x * x, axis=-1, keepdims=True)
    return x * lax.rsqrt(ms + EPS) * g


def _dot(a, b):
    return jnp.dot(a, b, preferred_element_type=F32)


def _dot_nt(a, b):
    return lax.dot_general(a, b, (((1,), (1,)), ((), ())), preferred_element_type=F32)


def _bias_from_distance(n, value_of_bucket):
    val = jnp.zeros(n.shape, F32) + value_of_bucket(0)
    for b in range(1, NUM_BUCKETS):
        val = jnp.where(n >= BUCKET_START[b], value_of_bucket(b), val)
    return val


def _lambda_full(lam_ref):
    s1 = jnp.sum(lam_ref[0:1, :] * lam_ref[1:2, :], axis=-1, keepdims=True)
    s2 = jnp.sum(lam_ref[2:3, :] * lam_ref[3:4, :], axis=-1, keepdims=True)
    return jnp.exp(s1) - jnp.exp(s2) + LAMBDA_INIT


def _inproj_kernel(x_ref, g1_ref, win_ref, lng_ref, lnb_ref, ws_ref, bs_ref, *out_refs, cs, prompt):
    if prompt:
        a_ref, q_ref, kf_ref, vf_ref, kb_ref, vt_ref = out_refs
    else:
        a_ref, q_ref, kf_ref, vf_ref, vn_ref = out_refs
    tm = x_ref.shape[0]
    xn0 = _rms(x_ref[0:tm // 2, :], g1_ref[...]).astype(BF16)
    v0 = _dot(xn0, win_ref[:, D_A:2 * D_A])
    xn1 = _rms(x_ref[tm // 2:tm, :], g1_ref[...]).astype(BF16)
    v = jnp.concatenate([v0, _dot(xn1, win_ref[:, D_A:2 * D_A])], axis=0)
    xn = jnp.concatenate([xn0, xn1], axis=0)
    vn16 = []
    for g in range(G_A):
        gs = slice(g * CH_A, (g + 1) * CH_A)
        vg = v[:, gs]
        mu = jnp.mean(vg, axis=-1, keepdims=True)
        d = vg - mu
        var = jnp.mean(d * d, axis=-1, keepdims=True)
        vn = d * lax.rsqrt(var + EPS) * lng_ref[:, gs] + lnb_ref[:, gs]
        if not prompt:
            vn_ref[:, gs] = vn
        vn16.append(vn.astype(BF16))

    vb = _dot(xn, win_ref[:, 2 * D_A + 2 * D_QK:])
    for hh in range(H_B):
        vf_ref[pl.ds(hh, tm, stride=H_B), :] = vb[:, hh * LANES:(hh + 1) * LANES]
    if prompt:
        vt_ref[0] = vb.T.astype(BF16)

    q = _dot(xn, win_ref[:, 2 * D_A:2 * D_A + D_QK])
    if prompt:
        q_ref[0] = (q * (SCALE * LOG2E)).T.astype(BF16)
    else:
        q_ref[...] = (q * SCALE).astype(BF16)

    u = _dot(xn, win_ref[:, 0:D_A])
    row = lax.broadcasted_iota(jnp.int32, (cs, cs), 0)
    col = lax.broadcasted_iota(jnp.int32, (cs, cs), 1)
    for g in range(G_A):
        gs = slice(g * CH_A, (g + 1) * CH_A)
        w = jnp.where(row >= col, ws_ref[g], 0.0).astype(BF16)
        for c in range(tm // cs):
            rs = slice(c * cs, (c + 1) * cs)
            s = _dot(w, vn16[g][rs, :]) + bs_ref[:, gs]
            a_ref[rs, gs] = (u[rs, gs] * s).astype(BF16)

    k = _dot(xn, win_ref[:, 2 * D_A + D_QK:2 * D_A + 2 * D_QK])
    for hh in range(H_B):
        kf_ref[pl.ds(hh, tm, stride=H_B), :] = k[:, hh * LANES:(hh + 1) * LANES]
    if prompt:
        kb_ref[...] = k.astype(BF16)


def _in_proj(x, g1, win16, lng, lnb, ws, bs_full, *, tm, cs, seq=None):
    t, d = x.shape
    d_in = win16.shape[1]
    const = lambda i: (0, 0)
    tile = lambda w: pl.BlockSpec((tm, w), lambda i: (i, 0))
    heads = pl.BlockSpec((tm * H_B, LANES), lambda i: (i, 0))
    if seq is not None:
        tiles = seq // tm
        transposed = pl.BlockSpec((1, D_QK, tm), lambda i: (i // tiles, 0, i % tiles))
        transposed_shape = jax.ShapeDtypeStruct((t // seq, D_QK, seq), BF16)
    out_shape = [jax.ShapeDtypeStruct((t, D_A), BF16),
                 jax.ShapeDtypeStruct((t, D_QK), BF16) if seq is None else transposed_shape,
                 jax.ShapeDtypeStruct((t * H_B, LANES), F32), jax.ShapeDtypeStruct((t * H_B, LANES), F32)]
    out_specs = [tile(D_A), tile(D_QK) if seq is None else transposed, heads, heads]
    if seq is not None:
        out_shape += [jax.ShapeDtypeStruct((t, D_QK), BF16), transposed_shape]
        out_specs += [tile(D_QK), transposed]
    else:
        out_shape.append(jax.ShapeDtypeStruct((t, D_A), F32))
        out_specs.append(tile(D_A))
    return pl.pallas_call(
        functools.partial(_inproj_kernel, cs=cs, prompt=seq is not None),
        grid=(t // tm,),
        in_specs=[tile(d), pl.BlockSpec((1, d), const), pl.BlockSpec((d, d_in), const),
                  pl.BlockSpec((1, D_A), const), pl.BlockSpec((1, D_A), const),
                  pl.BlockSpec((G_A, cs, cs), lambda i: (0, 0, 0)), pl.BlockSpec((cs, D_A), const)],
        out_specs=out_specs,
        out_shape=out_shape,
        compiler_params=pltpu.CompilerParams(dimension_semantics=("arbitrary",),
                                             vmem_limit_bytes=VMEM_LIMIT),
        name="in_proj",
    )(x, g1, win16, lng, lnb, ws, bs_full)


def _attn_kernel(rb_ref, q_ref, k_ref, vt_ref, lam_ref, sg_ref, o_ref,
                 diag_scr, corner_scr, qzt_scr, vt1_scr, sa_scr, sb_scr, sc_scr, mxa_scr, mxb_scr, mxc_scr,
                 m_scr, acc_scr, *, t):
    b, hg = pl.program_id(0), pl.program_id(1)
    hp = vt1_scr.shape[0]
    nq = k_ref.shape[1] // t
    c = LANES

    @pl.when((b == 0) & (hg == 0))
    def _build_bias_tiles():
        d0 = lax.broadcasted_iota(jnp.int32, (t, t), 1) - lax.broadcasted_iota(jnp.int32, (t, t), 0)
        dc = c + lax.broadcasted_iota(jnp.int32, (c, c), 1) - lax.broadcasted_iota(jnp.int32, (c, c), 0)
        for hh in range(H_B):
            near = _bias_from_distance(jnp.maximum(d0, 0), lambda bk: rb_ref[bk, hh]) * LOG2E
            diag_scr[hh] = jnp.where(d0 >= 0, near, NEG)
            corner_scr[hh] = (_bias_from_distance(dc, lambda bk: rb_ref[bk, hh])
                              - rb_ref[NUM_BUCKETS - 1, hh]) * LOG2E

    def stage_values():
        for e in range(hp):
            vt1_scr[e, 0:DV, :] = vt_ref[0, e * DV:(e + 1) * DV, :]
            vt1_scr[e, DV:DV + ONES, :] = jnp.ones((ONES, vt1_scr.shape[2]), BF16)

    def stage_queries(slot, tile):
        depth = lax.broadcasted_iota(jnp.int32, (LANES, t), 0)
        for e in range(hp):
            qt = q_ref[0, e * LANES:(e + 1) * LANES, pl.ds(pl.multiple_of(tile * t, t), t)]
            zero = jnp.zeros_like(qt)
            qzt_scr[slot, e] = jnp.concatenate([jnp.where(depth < DH, qt, zero),
                                                jnp.where(depth >= DH, qt, zero)], axis=1)

    def reset_state():
        for e in range(hp):
            m_scr[e] = jnp.full(m_scr.shape[1:], -jnp.inf, F32)
            acc_scr[e] = jnp.zeros(acc_scr.shape[1:], F32)

    def issue(buf, slot, kj):
        s_scr, mx_scr = buf
        for e in range(hp):
            k = k_ref[0, pl.ds(pl.multiple_of(kj * t, t), t), e * LANES:(e + 1) * LANES]
            s = _dot(k, qzt_scr[slot, e])
            s_scr[e] = s
            mx_scr[e] = jnp.max(s, axis=0, keepdims=True)

    def accumulate(e, s, mx, shift, kj):
        m_old = m_scr[e]
        m_new = jnp.maximum(m_old, mx + shift)
        alpha = jnp.exp2(m_old - m_new)
        p = jnp.exp2((s - (m_new - shift)).astype(BF16))
        vt1 = vt1_scr[e, :, pl.ds(pl.multiple_of(kj * t, t), t)]
        acc_scr[e] = alpha * acc_scr[e] + _dot(vt1, p)
        m_scr[e] = m_new

    def consume(buf, kj, qi):
        s_scr, mx_scr = buf
        near = jnp.where(kj == qi - 1, 1.0, 0.0).astype(F32)
        for e in range(hp):
            hh = hg * hp + e
            fix = near * corner_scr[hh]
            c0 = s_scr[e, t - c:t, 0:c] + fix
            c1 = s_scr[e, t - c:t, t:t + c] + fix
            s_scr[e, t - c:t, 0:c] = c0
            s_scr[e, t - c:t, t:t + c] = c1
            mx = mx_scr[e]
            mx = jnp.concatenate([jnp.maximum(mx[:, 0:c], jnp.max(c0, axis=0, keepdims=True)), mx[:, c:t],
                                  jnp.maximum(mx[:, t:t + c], jnp.max(c1, axis=0, keepdims=True)),
                                  mx[:, t + c:]], axis=1)
            accumulate(e, s_scr[e], mx, rb_ref[NUM_BUCKETS - 1, hh] * LOG2E, kj)

    def finish_tile(buf, qi):
        lam = _lambda_full(lam_ref)
        for e in range(hp):
            bias = diag_scr[hg * hp + e]
            s = buf[0][e]
            s = jnp.concatenate([s[:, 0:t] + bias, s[:, t:2 * t] + bias], axis=1)
            accumulate(e, s, jnp.max(s, axis=0, keepdims=True), 0.0, qi)
        for e in range(hp):
            ot = acc_scr[e, 0:DV, :] / acc_scr[e, DV:DV + 1, :]
            ot = ot[:, 0:t] - lam * ot[:, t:2 * t]
            ms = jnp.mean(ot * ot, axis=0, keepdims=True)
            o_ref[0, e * DV:(e + 1) * DV, pl.ds(pl.multiple_of(qi * t, t), t)] = (
                ot * lax.rsqrt(ms + EPS) * sg_ref[...] * (1.0 - LAMBDA_INIT)).astype(o_ref.dtype)
        reset_state()

    buf_a, buf_b, buf_c = (sa_scr, mxa_scr), (sb_scr, mxb_scr), (sc_scr, mxc_scr)

    def prefetch_next_tile(qi):
        stage_queries(1, jnp.minimum(qi + 1, nq - 1))
        issue(buf_c, 1, 0)

    def tile(qi, carry):
        @pl.when(qi == 0)
        def _first_tile():
            finish_tile(buf_c, qi)
            stage_queries(0, 1)
            issue(buf_c, 0, 0)

        @pl.when(qi > 0)
        def _first_block():
            issue(buf_a, 0, 1)
            consume(buf_c, 0, qi)

        def block_pair(i, inner):
            issue(buf_b, 0, 2 * i + 2)
            consume(buf_a, 2 * i + 1, qi)
            issue(buf_a, 0, 2 * i + 3)
            consume(buf_b, 2 * i + 2, qi)
            return inner

        lax.fori_loop(0, lax.shift_right_arithmetic(qi - 1, 1), block_pair, 0)

        @pl.when((qi > 0) & (qi % 2 == 1))
        def _diagonal_in_a():
            prefetch_next_tile(qi)
            finish_tile(buf_a, qi)
            qzt_scr[0] = qzt_scr[1]

        @pl.when((qi > 0) & (qi % 2 == 0))
        def _diagonal_in_b():
            issue(buf_b, 0, qi)
            consume(buf_a, qi - 1, qi)
            prefetch_next_tile(qi)
            finish_tile(buf_b, qi)
            qzt_scr[0] = qzt_scr[1]

        return carry

    stage_queries(0, 0)
    issue(buf_c, 0, 0)
    stage_values()
    reset_state()
    lax.fori_loop(0, nq, tile, 0)


def _prompt_attention(rel_bias, qt, k, vt, lam, sg_col, *, t, hp):
    b, s, _ = k.shape
    assert s % t == 0 and s // t >= 2 and H_B % hp == 0
    const = lambda bi, hi: (0, 0)
    w = hp * LANES
    transposed = pl.BlockSpec((1, w, s), lambda bi, hi: (bi, hi, 0))
    scores = pltpu.VMEM((hp, t, 2 * t), F32)
    per_query = pltpu.VMEM((hp, 1, 2 * t), F32)
    return pl.pallas_call(
        functools.partial(_attn_kernel, t=t),
        grid=(b, H_B // hp),
        in_specs=[pl.BlockSpec(memory_space=pltpu.SMEM), transposed,
                  pl.BlockSpec((1, s, w), lambda bi, hi: (bi, 0, hi)), transposed,
                  pl.BlockSpec((4, DH), const), pl.BlockSpec((DV, 1), const)],
        out_specs=transposed,
        out_shape=jax.ShapeDtypeStruct((b, D_B, s), BF16),
        scratch_shapes=[pltpu.VMEM((H_B, t, t), F32), pltpu.VMEM((H_B, LANES, LANES), F32),
                        pltpu.VMEM((2, hp, LANES, 2 * t), BF16), pltpu.VMEM((hp, DV + ONES, s), BF16),
                        scores, scores, scores, per_query, per_query, per_query,
                        per_query, pltpu.VMEM((hp, DV + ONES, 2 * t), F32)],
        compiler_params=pltpu.CompilerParams(dimension_semantics=("arbitrary",) * 2,
                                             vmem_limit_bytes=VMEM_LIMIT),
        name="prompt_attn",
    )(rel_bias, qt, k, vt, lam, sg_col)


def _paged_kernel(pt_ref, rb_ref, q_ref, kn_ref, vn_ref, lam_ref, sg_ref, ck_hbm, cv_hbm, o_ref,
                  qh_scr, new_scr, tile_scr, far_scr, m_scr, l_scr, acc_scr, kbuf, vbuf, k_sem, v_sem,
                  *, pages, dec):
    b, st = pl.program_id(0), pl.program_id(1)
    n_st = pl.num_programs(1)
    hrows = 2 * dec
    rows = H_B * hrows
    lin = b * n_st + st
    n_lin = pl.num_programs(0) * n_st
    prows = PAGE * H_B

    def page_copies(group, slot, p):
        page = pt_ref[group * pages + p]
        dst = pl.ds(p * prows, prows)
        return (pltpu.make_async_copy(ck_hbm.at[page], kbuf.at[slot, dst], k_sem.at[slot]),
                pltpu.make_async_copy(cv_hbm.at[page], vbuf.at[slot, dst], v_sem.at[slot]))

    def start_group(group):
        slot = group % RING
        for p in range(pages):
            for cp in page_copies(group, slot, p):
                cp.start()

    @pl.when(lin == 0)
    def _prime_ring():
        for g in range(RING - 1):
            start_group(g)

    @pl.when(lin + RING - 1 < n_lin)
    def _fetch_ahead():
        start_group(lin + RING - 1)

    slot = lin % RING
    for p in range(pages):
        for cp in page_copies(lin, slot, p):
            cp.wait()

    @pl.when((b == 0) & (st == 0))
    def _build_bias_tiles():
        r = lax.broadcasted_iota(jnp.int32, (rows, PAGE), 0)
        j = lax.broadcasted_iota(jnp.int32, (rows, PAGE), 1)
        head, i = r // hrows, r % dec

        def row_value(bk):
            val = jnp.zeros((rows, PAGE), F32) + rb_ref[bk, H_B - 1]
            for hh in range(H_B - 2, -1, -1):
                val = jnp.where(head == hh, rb_ref[bk, hh], val)
            return val

        tile_scr[0] = _bias_from_distance(PAGE + i - j, row_value)
        tile_scr[1] = jnp.where(j <= i, _bias_from_distance(jnp.maximum(i - j, 0), row_value), NEG)
        far_scr[...] = row_value(NUM_BUCKETS - 1)[:, 0:1]
        new_scr[...] = jnp.zeros_like(new_scr)

    @pl.when(st == 0)
    def _start_batch():
        r = lax.broadcasted_iota(jnp.int32, (hrows, LANES), 0)
        lane = lax.broadcasted_iota(jnp.int32, (hrows, LANES), 1)
        for hh in range(H_B):
            qh = q_ref[0, :, hh * LANES:(hh + 1) * LANES].astype(F32)
            qh = jnp.concatenate([qh, qh], axis=0)
            qh_scr[hh] = jnp.where(lane // DH == r // dec, qh, 0.0).astype(BF16)
        m_scr[...] = jnp.full_like(m_scr, -jnp.inf)
        l_scr[...] = jnp.zeros_like(l_scr)
        acc_scr[...] = jnp.zeros_like(acc_scr)

    def scores(head_keys):
        return jnp.concatenate([_dot_nt(qh_scr[hh], head_keys(hh)) for hh in range(H_B)], axis=0)

    def update(s, shift, head_values):
        m_old = m_scr[...]
        m_new = jnp.maximum(m_old, jnp.max(s, axis=-1, keepdims=True) + shift)
        alpha = jnp.exp(m_old - m_new)
        p = jnp.exp(s - (m_new - shift))
        l_scr[...] = alpha * l_scr[...] + jnp.sum(p, axis=-1, keepdims=True)
        p = p.astype(BF16)
        pv = jnp.concatenate([_dot(p[hh * hrows:(hh + 1) * hrows], head_values(hh))
                              for hh in range(H_B)], axis=0)
        acc_scr[...] = alpha * acc_scr[...] + pv
        m_scr[...] = m_new

    def cached(buf, first, last):
        return lambda hh: buf[slot, pl.ds(first * prows + hh, (last - first) * PAGE, stride=H_B), :].astype(BF16)

    def fresh(which):
        return lambda hh: new_scr[which, :, hh * LANES:(hh + 1) * LANES].astype(BF16)

    @pl.when(st < n_st - 1)
    def _far_pages():
        update(scores(cached(kbuf, 0, pages)), far_scr[...], cached(vbuf, 0, pages))

    @pl.when(st == n_st - 1)
    def _last_pages_and_new_rows():
        if pages > 1:
            update(scores(cached(kbuf, 0, pages - 1)), far_scr[...], cached(vbuf, 0, pages - 1))
        update(scores(cached(kbuf, pages - 1, pages)) + tile_scr[0], 0.0, cached(vbuf, pages - 1, pages))
        for hh in range(H_B):
            new_scr[0, 0:dec, hh * LANES:(hh + 1) * LANES] = kn_ref[0, pl.ds(hh, dec, stride=H_B), :]
            new_scr[1, 0:dec, hh * LANES:(hh + 1) * LANES] = vn_ref[0, pl.ds(hh, dec, stride=H_B), :]
        update(scores(fresh(0)) + tile_scr[1], 0.0, fresh(1))

        o = acc_scr[...] / l_scr[...]
        lam = _lambda_full(lam_ref)
        for hh in range(H_B):
            r0 = hh * hrows
            oh = o[r0:r0 + dec, :] - lam * o[r0 + dec:r0 + hrows, :]
            o_ref[0, :, hh * DV:(hh + 1) * DV] = (_rms(oh, sg_ref[...])
                                                  * (1.0 - LAMBDA_INIT)).astype(o_ref.dtype)


def _sample_attention(page_table, rel_bias, q, k_new, v_new, cache_k, cache_v, lam, sg, *, pages):
    nb, dec, _ = q.shape
    n_pages = page_table.shape[1]
    n_st = n_pages // pages
    rows = 2 * H_B * dec
    const2 = lambda bi, si, pt: (0, 0)
    per_batch = pl.BlockSpec((1, dec, D_QK), lambda bi, si, pt: (bi, 0, 0))
    new_rows = pl.BlockSpec((1, dec * H_B, LANES), lambda bi, si, pt: (bi, 0, 0))
    assert n_pages % pages == 0 and nb * n_st >= RING
    ring = pltpu.VMEM((RING, pages * PAGE * H_B, LANES), F32)

    grid_spec = pltpu.PrefetchScalarGridSpec(
        num_scalar_prefetch=1,
        grid=(nb, n_st),
        in_specs=[pl.BlockSpec(memory_space=pltpu.SMEM), per_batch, new_rows, new_rows,
                  pl.BlockSpec((4, DH), const2), pl.BlockSpec((1, DV), const2),
                  pl.BlockSpec(memory_space=pl.ANY), pl.BlockSpec(memory_space=pl.ANY)],
        out_specs=per_batch,
        scratch_shapes=[pltpu.VMEM((H_B, 2 * dec, LANES), BF16), pltpu.VMEM((2, PAGE, D_QK), F32),
                        pltpu.VMEM((2, rows, PAGE), F32), pltpu.VMEM((rows, 1), F32),
                        pltpu.VMEM((rows, 1), F32), pltpu.VMEM((rows, 1), F32),
                        pltpu.VMEM((rows, DV), F32), ring, ring,
                        pltpu.SemaphoreType.DMA((RING,)), pltpu.SemaphoreType.DMA((RING,))],
    )
    return pl.pallas_call(
        functools.partial(_paged_kernel, pages=pages, dec=dec),
        grid_spec=grid_spec,
        out_shape=jax.ShapeDtypeStruct((nb, dec, D_B), BF16),
        compiler_params=pltpu.CompilerParams(dimension_semantics=("arbitrary", "arbitrary"),
                                             vmem_limit_bytes=VMEM_LIMIT),
        name="paged_attn",
    )(page_table.reshape(-1), rel_bias, q, k_new, v_new, lam, sg, cache_k, cache_v)


def _tail_kernel(x_ref, a_ref, b_ref, wo_ref, g2_ref, wg_ref, wu_ref, wd_ref, gf_ref, y_ref, *, b_transposed):
    tm = x_ref.shape[0]
    halves = [slice(p * tm // TAIL_PARTS, (p + 1) * tm // TAIL_PARTS) for p in range(TAIL_PARTS)]
    hs = []
    for rs in halves:
        if b_transposed:
            bw = lax.dot_general(b_ref[0, :, rs], wo_ref[D_A:, :], (((0,), (0,)), ((), ())),
                                 preferred_element_type=F32)
        else:
            bw = _dot(b_ref[rs, :], wo_ref[D_A:, :])
        hs.append(x_ref[rs, :] + _dot(a_ref[rs, :], wo_ref[0:D_A, :]) + bw)
    fs = []
    for h in hs:
        hn = _rms(h, g2_ref[...]).astype(BF16)
        gate = _dot(hn, wg_ref[...])
        fs.append((gate * (1.0 / (1.0 + jnp.exp(-gate))) * _dot(hn, wu_ref[...])).astype(BF16))
    for rs, h, f in zip(halves, hs, fs):
        y_ref[rs, :] = _rms(h + _dot(f, wd_ref[...]), gf_ref[...])


def _tail(x, a, bmix, wo16, g2, wg16, wu16, wd16, gf, *, tm):
    t, d = x.shape
    d_ff = wg16.shape[1]
    const = lambda i: (0, 0)
    resident = lambda shape: pl.BlockSpec(shape, const, pipeline_mode=pl.Buffered(1))
    b_transposed = bmix.ndim == 3
    if b_transposed:
        tiles = bmix.shape[2] // tm
        b_spec = pl.BlockSpec((1, D_B, tm), lambda i: (i // tiles, 0, i % tiles))
    else:
        b_spec = pl.BlockSpec((tm, D_B), lambda i: (i, 0))
    return pl.pallas_call(
        functools.partial(_tail_kernel, b_transposed=b_transposed),
        grid=(t // tm,),
        in_specs=[pl.BlockSpec((tm, d), lambda i: (i, 0)), pl.BlockSpec((tm, D_A), lambda i: (i, 0)),
                  b_spec, resident((D_A + D_B, d)),
                  pl.BlockSpec((1, d), const), resident((d, d_ff)), resident((d, d_ff)),
                  resident((d_ff, d)), pl.BlockSpec((1, d), const)],
        out_specs=pl.BlockSpec((tm, d), lambda i: (i, 0)),
        out_shape=jax.ShapeDtypeStruct((t, d), F32),
        compiler_params=pltpu.CompilerParams(dimension_semantics=("arbitrary",),
                                             vmem_limit_bytes=VMEM_LIMIT),
        name="tail",
    )(x, a, bmix, wo16, g2, wg16, wu16, wd16, gf)


def kernel(x_prompt, x_sample, cache_k, cache_v, page_table, norm1_g, w_in, gmlp_ln_g, gmlp_ln_b, gmlp_ws, gmlp_bs, lambda_q1, lambda_k1, lambda_q2, lambda_k2, subln_g, rel_bias, w_out, norm2_g, w_gate, w_up, w_down, final_g):
    nb, seq, d = x_prompt.shape
    db, dec, _ = x_sample.shape
    row = lambda p: p.reshape(1, -1)

    win16, wo16 = w_in.astype(BF16), w_out.astype(BF16)
    wg16, wu16, wd16 = w_gate.astype(BF16), w_up.astype(BF16), w_down.astype(BF16)
    lam = jnp.stack([lambda_q1, lambda_k1, lambda_q2, lambda_k2])
    proj_params = (row(norm1_g), win16, row(gmlp_ln_g), row(gmlp_ln_b))
    tail_params = (wo16, row(norm2_g), wg16, wu16, wd16, row(final_g))

    xp = x_prompt.reshape(nb * seq, d)
    bs_p = jnp.repeat(gmlp_bs.T, CH_A, axis=1)
    a_p, qt_p, kf_p, vf_p, kb_p, vt_p = _in_proj(xp, *proj_params, gmlp_ws, bs_p,
                                                tm=1024, cs=CHUNK, seq=seq)
    bt_p = _prompt_attention(rel_bias, qt_p, kb_p.reshape(nb, seq, D_QK), vt_p, lam,
                             subln_g.reshape(DV, 1), t=512, hp=2)
    y_p = _tail(xp, a_p, bt_p, *tail_params, tm=512)

    xs = x_sample.reshape(db * dec, d)
    pos = np.arange(db * dec)
    pick = jnp.asarray(pos[:, None] % dec == np.arange(dec)[None, :], F32)
    same = jnp.asarray(pos[:, None] // dec == pos[None, :] // dec)
    ws_s = jnp.where(same, jnp.einsum('rt,gts,cs->grc', pick, gmlp_ws[:, :dec, :dec], pick,
                                      precision=lax.Precision.HIGHEST), 0.0)
    bs_s = jnp.tile(jnp.repeat(gmlp_bs[:, :dec].T, CH_A, axis=1), (db, 1))
    a_s, q_s, kf_s, vf_s, gv_s = _in_proj(xs, *proj_params, ws_s, bs_s, tm=db * dec, cs=db * dec)
    b_s = _sample_attention(page_table, rel_bias, q_s.reshape(db, dec, D_QK),
                            kf_s.reshape(db, dec * H_B, LANES), vf_s.reshape(db, dec * H_B, LANES),
                            cache_k.reshape(-1, PAGE * H_B, DV), cache_v.reshape(-1, PAGE * H_B, DV),
                            lam, row(subln_g), pages=16)
    y_s = _tail(xs, a_s, b_s.reshape(db * dec, D_B), *tail_params, tm=db * dec)

    return (y_p.reshape(nb, seq, d), y_s.reshape(db, dec, d),
            kf_p.reshape(nb, seq, H_B, 2 * DH), vf_p.reshape(nb, seq, H_B, DV),
            kf_s.reshape(db, dec, H_B, 2 * DH), vf_s.reshape(db, dec, H_B, DV),
            gv_s.reshape(db, dec, D_A))
```

```python
import functools
import math

import numpy as np
import jax
import jax.numpy as jnp
from jax import lax
from jax.experimental import pallas as pl
from jax.experimental.pallas import tpu as pltpu

F32 = jnp.float32
BF16 = jnp.bfloat16

EPS = 1e-6
LANES = 128
G_A = 4
CH_A = 128
CHUNK = 128
H_B = 4
DV = 128
DH = 64
D_A = G_A * CH_A
D_QK = H_B * 2 * DH
D_B = H_B * DV
SCALE = DH ** -0.5
LOG2E = math.log2(math.e)
LAYER_INDEX = 1
LAMBDA_INIT = 0.8 - 0.6 * math.exp(-0.3 * (LAYER_INDEX - 1))
NUM_BUCKETS = 32
MAX_EXACT = NUM_BUCKETS // 2
MAX_DISTANCE = 128
PAGE = 128
TAIL_PARTS = 2
CAST_STEPS = 16
RING = 3
ONES = 16
NEG = -1e30
VMEM_LIMIT = 56 * 1024 * 1024


def _bucket_thresholds():
    n = np.arange(0, 2 * MAX_DISTANCE)
    nf = np.maximum(n, 1).astype(np.float64)
    large = MAX_EXACT + (np.log(nf / MAX_EXACT) / math.log(MAX_DISTANCE / MAX_EXACT)
                         * (NUM_BUCKETS - MAX_EXACT)).astype(np.int32)
    bucket = np.where(n < MAX_EXACT, n, np.minimum(large, NUM_BUCKETS - 1))
    assert np.all(np.diff(bucket) >= 0) and bucket[MAX_DISTANCE - 1] == NUM_BUCKETS - 1
    return [int(n[bucket >= b].min()) for b in range(NUM_BUCKETS)]


BUCKET_START = _bucket_thresholds()
FAR = BUCKET_START[-1]
assert FAR <= LANES and FAR <= PAGE


def _rms(x, g):
    ms = jnp.mean(x * x, axis=-1, keepdims=True)
    return x * lax.rsqrt(ms + EPS) * g


def _dot(a, b):
    return jnp.dot(a, b, preferred_element_type=F32)


def _dot_nt(a, b):
    return lax.dot_general(a, b, (((1,), (1,)), ((), ())), preferred_element_type=F32)


def _bias_from_distance(n, value_of_bucket):
    val = jnp.zeros(n.shape, F32) + value_of_bucket(0)
    for b in range(1, NUM_BUCKETS):
        val = jnp.where(n >= BUCKET_START[b], value_of_bucket(b), val)
    return val


def _lambda_full(lam_ref):
    s1 = jnp.sum(lam_ref[0:1, :] * lam_ref[1:2, :], axis=-1, keepdims=True)
    s2 = jnp.sum(lam_ref[2:3, :] * lam_ref[3:4, :], axis=-1, keepdims=True)
    return jnp.exp(s1) - jnp.exp(s2) + LAMBDA_INIT


def _inproj_kernel(x_ref, g1_ref, win_ref, lng_ref, lnb_ref, ws_ref, bs_ref, *rest, cs, prompt, n_cast):
    cast_in, out_refs = rest[:n_cast], rest[n_cast:]
    cast_out, out_refs = out_refs[len(out_refs) - n_cast:], out_refs[:len(out_refs) - n_cast]
    if n_cast:
        @pl.when(pl.program_id(0) < CAST_STEPS)
        def _cast_weight_rows():
            for src, dst in zip(cast_in, cast_out):
                dst[...] = src[...].astype(BF16)

    if prompt:
        a_ref, q_ref, kf_ref, vf_ref, kb_ref, vt_ref = out_refs
    else:
        a_ref, q_ref, kf_ref, vf_ref, vn_ref = out_refs
    tm = x_ref.shape[0]
    xn0 = _rms(x_ref[0:tm // 2, :], g1_ref[...]).astype(BF16)
    v0 = _dot(xn0, win_ref[:, D_A:2 * D_A])
    xn1 = _rms(x_ref[tm // 2:tm, :], g1_ref[...]).astype(BF16)
    v = jnp.concatenate([v0, _dot(xn1, win_ref[:, D_A:2 * D_A])], axis=0)
    xn = jnp.concatenate([xn0, xn1], axis=0)
    vn16 = []
    for g in range(G_A):
        gs = slice(g * CH_A, (g + 1) * CH_A)
        vg = v[:, gs]
        mu = jnp.mean(vg, axis=-1, keepdims=True)
        d = vg - mu
        var = jnp.mean(d * d, axis=-1, keepdims=True)
        vn = d * lax.rsqrt(var + EPS) * lng_ref[:, gs] + lnb_ref[:, gs]
        if not prompt:
            vn_ref[:, gs] = vn
        vn16.append(vn.astype(BF16))

    vb = _dot(xn, win_ref[:, 2 * D_A + 2 * D_QK:])
    for hh in range(H_B):
        vf_ref[pl.ds(hh, tm, stride=H_B), :] = vb[:, hh * LANES:(hh + 1) * LANES]
    if prompt:
        vt_ref[0] = vb.T.astype(BF16)

    q = _dot(xn, win_ref[:, 2 * D_A:2 * D_A + D_QK])
    if prompt:
        q_ref[0] = (q * (SCALE * LOG2E)).T.astype(BF16)
    else:
        q_ref[...] = (q * SCALE).astype(BF16)

    u = _dot(xn, win_ref[:, 0:D_A])
    row = lax.broadcasted_iota(jnp.int32, (cs, cs), 0)
    col = lax.broadcasted_iota(jnp.int32, (cs, cs), 1)
    for g in range(G_A):
        gs = slice(g * CH_A, (g + 1) * CH_A)
        w = jnp.where(row >= col, ws_ref[g], 0.0).astype(BF16)
        for c in range(tm // cs):
            rs = slice(c * cs, (c + 1) * cs)
            s = _dot(w, vn16[g][rs, :]) + bs_ref[:, gs]
            a_ref[rs, gs] = (u[rs, gs] * s).astype(BF16)

    k = _dot(xn, win_ref[:, 2 * D_A + D_QK:2 * D_A + 2 * D_QK])
    for hh in range(H_B):
        kf_ref[pl.ds(hh, tm, stride=H_B), :] = k[:, hh * LANES:(hh + 1) * LANES]
    if prompt:
        kb_ref[...] = k.astype(BF16)


def _in_proj(x, g1, win16, lng, lnb, ws, bs_full, *, tm, cs, seq=None, cast=()):
    t, d = x.shape
    d_in = win16.shape[1]
    const = lambda i: (0, 0)
    tile = lambda w: pl.BlockSpec((tm, w), lambda i: (i, 0))
    heads = pl.BlockSpec((tm * H_B, LANES), lambda i: (i, 0))
    if seq is not None:
        tiles = seq // tm
        transposed = pl.BlockSpec((1, D_QK, tm), lambda i: (i // tiles, 0, i % tiles))
        transposed_shape = jax.ShapeDtypeStruct((t // seq, D_QK, seq), BF16)
    out_shape = [jax.ShapeDtypeStruct((t, D_A), BF16),
                 jax.ShapeDtypeStruct((t, D_QK), BF16) if seq is None else transposed_shape,
                 jax.ShapeDtypeStruct((t * H_B, LANES), F32), jax.ShapeDtypeStruct((t * H_B, LANES), F32)]
    out_specs = [tile(D_A), tile(D_QK) if seq is None else transposed, heads, heads]
    if seq is not None:
        out_shape += [jax.ShapeDtypeStruct((t, D_QK), BF16), transposed_shape]
        out_specs += [tile(D_QK), transposed]
    else:
        out_shape.append(jax.ShapeDtypeStruct((t, D_A), F32))
        out_specs.append(tile(D_A))
    cast_specs = []
    for w in cast:
        assert t // tm >= CAST_STEPS and w.shape[0] % (CAST_STEPS * 16) == 0
        cast_specs.append(pl.BlockSpec((w.shape[0] // CAST_STEPS, w.shape[1]),
                                       lambda i: (jnp.minimum(i, CAST_STEPS - 1), 0)))
        out_shape.append(jax.ShapeDtypeStruct(w.shape, BF16))
    return pl.pallas_call(
        functools.partial(_inproj_kernel, cs=cs, prompt=seq is not None, n_cast=len(cast)),
        grid=(t // tm,),
        in_specs=[tile(d), pl.BlockSpec((1, d), const), pl.BlockSpec((d, d_in), const),
                  pl.BlockSpec((1, D_A), const), pl.BlockSpec((1, D_A), const),
                  pl.BlockSpec((G_A, cs, cs), lambda i: (0, 0, 0)), pl.BlockSpec((cs, D_A), const)] + cast_specs,
        out_specs=out_specs + cast_specs,
        out_shape=out_shape,
        compiler_params=pltpu.CompilerParams(dimension_semantics=("arbitrary",),
                                             vmem_limit_bytes=VMEM_LIMIT),
        name="in_proj",
    )(x, g1, win16, lng, lnb, ws, bs_full, *cast)


def _attn_kernel(rb_ref, q_ref, k_ref, vt_ref, lam_ref, sg_ref, o_ref,
                 diag_scr, corner_scr, qzt_scr, vt1_scr, sa_scr, sb_scr, sc_scr, mxa_scr, mxb_scr, mxc_scr,
                 m_scr, acc_scr, *, t):
    b, hg = pl.program_id(0), pl.program_id(1)
    hp = vt1_scr.shape[0]
    nq = k_ref.shape[1] // t
    c = LANES

    @pl.when((b == 0) & (hg == 0))
    def _build_bias_tiles():
        d0 = lax.broadcasted_iota(jnp.int32, (t, t), 1) - lax.broadcasted_iota(jnp.int32, (t, t), 0)
        dc = c + lax.broadcasted_iota(jnp.int32, (c, c), 1) - lax.broadcasted_iota(jnp.int32, (c, c), 0)
        for hh in range(H_B):
            near = _bias_from_distance(jnp.maximum(d0, 0), lambda bk: rb_ref[bk, hh]) * LOG2E
            diag_scr[hh] = jnp.where(d0 >= 0, near, NEG)
            corner_scr[hh] = (_bias_from_distance(dc, lambda bk: rb_ref[bk, hh])
                              - rb_ref[NUM_BUCKETS - 1, hh]) * LOG2E

    def stage_values():
        for e in range(hp):
            vt1_scr[e, 0:DV, :] = vt_ref[0, e * DV:(e + 1) * DV, :]
            vt1_scr[e, DV:DV + ONES, :] = jnp.ones((ONES, vt1_scr.shape[2]), BF16)

    def stage_queries(slot, tile):
        depth = lax.broadcasted_iota(jnp.int32, (LANES, t), 0)
        for e in range(hp):
            qt = q_ref[0, e * LANES:(e + 1) * LANES, pl.ds(pl.multiple_of(tile * t, t), t)]
            zero = jnp.zeros_like(qt)
            qzt_scr[slot, e] = jnp.concatenate([jnp.where(depth < DH, qt, zero),
                                                jnp.where(depth >= DH, qt, zero)], axis=1)

    def reset_state():
        for e in range(hp):
            m_scr[e] = jnp.full(m_scr.shape[1:], -jnp.inf, F32)
            acc_scr[e] = jnp.zeros(acc_scr.shape[1:], F32)

    def issue(buf, slot, kj):
        s_scr, mx_scr = buf
        for e in range(hp):
            k = k_ref[0, pl.ds(pl.multiple_of(kj * t, t), t), e * LANES:(e + 1) * LANES]
            s = _dot(k, qzt_scr[slot, e])
            s_scr[e] = s
            mx_scr[e] = jnp.max(s, axis=0, keepdims=True)

    def accumulate(e, s, mx, shift, kj):
        m_old = m_scr[e]
        m_new = jnp.maximum(m_old, mx + shift)
        alpha = jnp.exp2(m_old - m_new)
        p = jnp.exp2((s - (m_new - shift)).astype(BF16))
        vt1 = vt1_scr[e, :, pl.ds(pl.multiple_of(kj * t, t), t)]
        acc_scr[e] = alpha * acc_scr[e] + _dot(vt1, p)
        m_scr[e] = m_new

    def consume(buf, kj, qi):
        s_scr, mx_scr = buf
        near = jnp.where(kj == qi - 1, 1.0, 0.0).astype(F32)
        for e in range(hp):
            hh = hg * hp + e
            fix = near * corner_scr[hh]
            c0 = s_scr[e, t - c:t, 0:c] + fix
            c1 = s_scr[e, t - c:t, t:t + c] + fix
            s_scr[e, t - c:t, 0:c] = c0
            s_scr[e, t - c:t, t:t + c] = c1
            mx = mx_scr[e]
            mx = jnp.concatenate([jnp.maximum(mx[:, 0:c], jnp.max(c0, axis=0, keepdims=True)), mx[:, c:t],
                                  jnp.maximum(mx[:, t:t + c], jnp.max(c1, axis=0, keepdims=True)),
                                  mx[:, t + c:]], axis=1)
            accumulate(e, s_scr[e], mx, rb_ref[NUM_BUCKETS - 1, hh] * LOG2E, kj)

    def finish_tile(buf, qi):
        lam = _lambda_full(lam_ref)
        for e in range(hp):
            bias = diag_scr[hg * hp + e]
            s = buf[0][e]
            s = jnp.concatenate([s[:, 0:t] + bias, s[:, t:2 * t] + bias], axis=1)
            accumulate(e, s, jnp.max(s, axis=0, keepdims=True), 0.0, qi)
        for e in range(hp):
            ot = acc_scr[e, 0:DV, :] / acc_scr[e, DV:DV + 1, :]
            ot = ot[:, 0:t] - lam * ot[:, t:2 * t]
            ms = jnp.mean(ot * ot, axis=0, keepdims=True)
            o_ref[0, e * DV:(e + 1) * DV, pl.ds(pl.multiple_of(qi * t, t), t)] = (
                ot * lax.rsqrt(ms + EPS) * sg_ref[...] * (1.0 - LAMBDA_INIT)).astype(o_ref.dtype)
        reset_state()

    buf_a, buf_b, buf_c = (sa_scr, mxa_scr), (sb_scr, mxb_scr), (sc_scr, mxc_scr)

    def prefetch_next_tile(qi):
        stage_queries(1, jnp.minimum(qi + 1, nq - 1))
        issue(buf_c, 1, 0)

    def tile(qi, carry):
        @pl.when(qi == 0)
        def _first_tile():
            finish_tile(buf_c, qi)
            stage_queries(0, 1)
            issue(buf_c, 0, 0)

        @pl.when(qi > 0)
        def _first_block():
            issue(buf_a, 0, 1)
            consume(buf_c, 0, qi)

        def block_pair(i, inner):
            issue(buf_b, 0, 2 * i + 2)
            consume(buf_a, 2 * i + 1, qi)
            issue(buf_a, 0, 2 * i + 3)
            consume(buf_b, 2 * i + 2, qi)
            return inner

        lax.fori_loop(0, lax.shift_right_arithmetic(qi - 1, 1), block_pair, 0)

        @pl.when((qi > 0) & (qi % 2 == 1))
        def _diagonal_in_a():
            prefetch_next_tile(qi)
            finish_tile(buf_a, qi)
            qzt_scr[0] = qzt_scr[1]

        @pl.when((qi > 0) & (qi % 2 == 0))
        def _diagonal_in_b():
            issue(buf_b, 0, qi)
            consume(buf_a, qi - 1, qi)
            prefetch_next_tile(qi)
            finish_tile(buf_b, qi)
            qzt_scr[0] = qzt_scr[1]

        return carry

    stage_queries(0, 0)
    issue(buf_c, 0, 0)
    stage_values()
    reset_state()
    lax.fori_loop(0, nq, tile, 0)


def _prompt_attention(rel_bias, qt, k, vt, lam, sg_col, *, t, hp):
    b, s, _ = k.shape
    assert s % t == 0 and s // t >= 2 and H_B % hp == 0
    const = lambda bi, hi: (0, 0)
    w = hp * LANES
    transposed = pl.BlockSpec((1, w, s), lambda bi, hi: (bi, hi, 0))
    scores = pltpu.VMEM((hp, t, 2 * t), F32)
    per_query = pltpu.VMEM((hp, 1, 2 * t), F32)
    return pl.pallas_call(
        functools.partial(_attn_kernel, t=t),
        grid=(b, H_B // hp),
        in_specs=[pl.BlockSpec(memory_space=pltpu.SMEM), transposed,
                  pl.BlockSpec((1, s, w), lambda bi, hi: (bi, 0, hi)), transposed,
                  pl.BlockSpec((4, DH), const), pl.BlockSpec((DV, 1), const)],
        out_specs=transposed,
        out_shape=jax.ShapeDtypeStruct((b, D_B, s), BF16),
        scratch_shapes=[pltpu.VMEM((H_B, t, t), F32), pltpu.VMEM((H_B, LANES, LANES), F32),
                        pltpu.VMEM((2, hp, LANES, 2 * t), BF16), pltpu.VMEM((hp, DV + ONES, s), BF16),
                        scores, scores, scores, per_query, per_query, per_query,
                        per_query, pltpu.VMEM((hp, DV + ONES, 2 * t), F32)],
        compiler_params=pltpu.CompilerParams(dimension_semantics=("arbitrary",) * 2,
                                             vmem_limit_bytes=VMEM_LIMIT),
        name="prompt_attn",
    )(rel_bias, qt, k, vt, lam, sg_col)


def _paged_kernel(pt_ref, rb_ref, q_ref, kn_ref, vn_ref, lam_ref, sg_ref, ck_hbm, cv_hbm, o_ref,
                  qh_scr, new_scr, tile_scr, far_scr, m_scr, l_scr, acc_scr, kbuf, vbuf, k_sem, v_sem,
                  *, pages, dec):
    b, st = pl.program_id(0), pl.program_id(1)
    n_st = pl.num_programs(1)
    hrows = 2 * dec
    rows = H_B * hrows
    lin = b * n_st + st
    n_lin = pl.num_programs(0) * n_st
    prows = PAGE * H_B

    def page_copies(group, slot, p):
        page = pt_ref[group * pages + p]
        dst = pl.ds(p * prows, prows)
        return (pltpu.make_async_copy(ck_hbm.at[page], kbuf.at[slot, dst], k_sem.at[slot]),
                pltpu.make_async_copy(cv_hbm.at[page], vbuf.at[slot, dst], v_sem.at[slot]))

    def start_group(group):
        slot = group % RING
        for p in range(pages):
            for cp in page_copies(group, slot, p):
                cp.start()

    @pl.when(lin == 0)
    def _prime_ring():
        for g in range(RING - 1):
            start_group(g)

    @pl.when(lin + RING - 1 < n_lin)
    def _fetch_ahead():
        start_group(lin + RING - 1)

    slot = lin % RING
    for p in range(pages):
        for cp in page_copies(lin, slot, p):
            cp.wait()

    @pl.when((b == 0) & (st == 0))
    def _build_bias_tiles():
        r = lax.broadcasted_iota(jnp.int32, (rows, PAGE), 0)
        j = lax.broadcasted_iota(jnp.int32, (rows, PAGE), 1)
        head, i = r // hrows, r % dec

        def row_value(bk):
            val = jnp.zeros((rows, PAGE), F32) + rb_ref[bk, H_B - 1]
            for hh in range(H_B - 2, -1, -1):
                val = jnp.where(head == hh, rb_ref[bk, hh], val)
            return val

        tile_scr[0] = _bias_from_distance(PAGE + i - j, row_value)
        tile_scr[1] = jnp.where(j <= i, _bias_from_distance(jnp.maximum(i - j, 0), row_value), NEG)
        far_scr[...] = row_value(NUM_BUCKETS - 1)[:, 0:1]
        new_scr[...] = jnp.zeros_like(new_scr)

    @pl.when(st == 0)
    def _start_batch():
        r = lax.broadcasted_iota(jnp.int32, (hrows, LANES), 0)
        lane = lax.broadcasted_iota(jnp.int32, (hrows, LANES), 1)
        for hh in range(H_B):
            qh = q_ref[0, :, hh * LANES:(hh + 1) * LANES].astype(F32)
            qh = jnp.concatenate([qh, qh], axis=0)
            qh_scr[hh] = jnp.where(lane // DH == r // dec, qh, 0.0).astype(BF16)
        m_scr[...] = jnp.full_like(m_scr, -jnp.inf)
        l_scr[...] = jnp.zeros_like(l_scr)
        acc_scr[...] = jnp.zeros_like(acc_scr)

    def scores(head_keys):
        return jnp.concatenate([_dot_nt(qh_scr[hh], head_keys(hh)) for hh in range(H_B)], axis=0)

    def update(s, shift, head_values):
        m_old = m_scr[...]
        m_new = jnp.maximum(m_old, jnp.max(s, axis=-1, keepdims=True) + shift)
        alpha = jnp.exp(m_old - m_new)
        p = jnp.exp(s - (m_new - shift))
        l_scr[...] = alpha * l_scr[...] + jnp.sum(p, axis=-1, keepdims=True)
        p = p.astype(BF16)
        pv = jnp.concatenate([_dot(p[hh * hrows:(hh + 1) * hrows], head_values(hh))
                              for hh in range(H_B)], axis=0)
        acc_scr[...] = alpha * acc_scr[...] + pv
        m_scr[...] = m_new

    def cached(buf, first, last):
        return lambda hh: buf[slot, pl.ds(first * prows + hh, (last - first) * PAGE, stride=H_B), :].astype(BF16)

    def fresh(which):
        return lambda hh: new_scr[which, :, hh * LANES:(hh + 1) * LANES].astype(BF16)

    @pl.when(st < n_st - 1)
    def _far_pages():
        update(scores(cached(kbuf, 0, pages)), far_scr[...], cached(vbuf, 0, pages))

    @pl.when(st == n_st - 1)
    def _last_pages_and_new_rows():
        if pages > 1:
            update(scores(cached(kbuf, 0, pages - 1)), far_scr[...], cached(vbuf, 0, pages - 1))
        update(scores(cached(kbuf, pages - 1, pages)) + tile_scr[0], 0.0, cached(vbuf, pages - 1, pages))
        for hh in range(H_B):
            new_scr[0, 0:dec, hh * LANES:(hh + 1) * LANES] = kn_ref[0, pl.ds(hh, dec, stride=H_B), :]
            new_scr[1, 0:dec, hh * LANES:(hh + 1) * LANES] = vn_ref[0, pl.ds(hh, dec, stride=H_B), :]
        update(scores(fresh(0)) + tile_scr[1], 0.0, fresh(1))

        o = acc_scr[...] / l_scr[...]
        lam = _lambda_full(lam_ref)
        for hh in range(H_B):
            r0 = hh * hrows
            oh = o[r0:r0 + dec, :] - lam * o[r0 + dec:r0 + hrows, :]
            o_ref[0, :, hh * DV:(hh + 1) * DV] = (_rms(oh, sg_ref[...])
                                                  * (1.0 - LAMBDA_INIT)).astype(o_ref.dtype)


def _sample_attention(page_table, rel_bias, q, k_new, v_new, cache_k, cache_v, lam, sg, *, pages):
    nb, dec, _ = q.shape
    n_pages = page_table.shape[1]
    n_st = n_pages // pages
    rows = 2 * H_B * dec
    const2 = lambda bi, si, pt: (0, 0)
    per_batch = pl.BlockSpec((1, dec, D_QK), lambda bi, si, pt: (bi, 0, 0))
    new_rows = pl.BlockSpec((1, dec * H_B, LANES), lambda bi, si, pt: (bi, 0, 0))
    assert n_pages % pages == 0 and nb * n_st >= RING
    ring = pltpu.VMEM((RING, pages * PAGE * H_B, LANES), F32)

    grid_spec = pltpu.PrefetchScalarGridSpec(
        num_scalar_prefetch=1,
        grid=(nb, n_st),
        in_specs=[pl.BlockSpec(memory_space=pltpu.SMEM), per_batch, new_rows, new_rows,
                  pl.BlockSpec((4, DH), const2), pl.BlockSpec((1, DV), const2),
                  pl.BlockSpec(memory_space=pl.ANY), pl.BlockSpec(memory_space=pl.ANY)],
        out_specs=per_batch,
        scratch_shapes=[pltpu.VMEM((H_B, 2 * dec, LANES), BF16), pltpu.VMEM((2, PAGE, D_QK), F32),
                        pltpu.VMEM((2, rows, PAGE), F32), pltpu.VMEM((rows, 1), F32),
                        pltpu.VMEM((rows, 1), F32), pltpu.VMEM((rows, 1), F32),
                        pltpu.VMEM((rows, DV), F32), ring, ring,
                        pltpu.SemaphoreType.DMA((RING,)), pltpu.SemaphoreType.DMA((RING,))],
    )
    return pl.pallas_call(
        functools.partial(_paged_kernel, pages=pages, dec=dec),
        grid_spec=grid_spec,
        out_shape=jax.ShapeDtypeStruct((nb, dec, D_B), BF16),
        compiler_params=pltpu.CompilerParams(dimension_semantics=("arbitrary", "arbitrary"),
                                             vmem_limit_bytes=VMEM_LIMIT),
        name="paged_attn",
    )(page_table.reshape(-1), rel_bias, q, k_new, v_new, lam, sg, cache_k, cache_v)


def _tail_kernel(x_ref, a_ref, b_ref, wo_ref, g2_ref, wg_ref, wu_ref, wd_ref, gf_ref, y_ref, *, b_transposed):
    tm = x_ref.shape[0]
    halves = [slice(p * tm // TAIL_PARTS, (p + 1) * tm // TAIL_PARTS) for p in range(TAIL_PARTS)]
    hs = []
    for rs in halves:
        if b_transposed:
            bw = lax.dot_general(b_ref[0, :, rs], wo_ref[D_A:, :], (((0,), (0,)), ((), ())),
                                 preferred_element_type=F32)
        else:
            bw = _dot(b_ref[rs, :], wo_ref[D_A:, :])
        hs.append(x_ref[rs, :] + _dot(a_ref[rs, :], wo_ref[0:D_A, :]) + bw)
    fs = []
    for h in hs:
        hn = _rms(h, g2_ref[...]).astype(BF16)
        gate = _dot(hn, wg_ref[...])
        fs.append((gate * (1.0 / (1.0 + jnp.exp(-gate))) * _dot(hn, wu_ref[...])).astype(BF16))
    for rs, h, f in zip(halves, hs, fs):
        y_ref[rs, :] = _rms(h + _dot(f, wd_ref[...]), gf_ref[...])


def _tail(x, a, bmix, wo16, g2, wg16, wu16, wd16, gf, *, tm):
    t, d = x.shape
    d_ff = wg16.shape[1]
    const = lambda i: (0, 0)
    resident = lambda shape: pl.BlockSpec(shape, const, pipeline_mode=pl.Buffered(1))
    b_transposed = bmix.ndim == 3
    if b_transposed:
        tiles = bmix.shape[2] // tm
        b_spec = pl.BlockSpec((1, D_B, tm), lambda i: (i // tiles, 0, i % tiles))
    else:
        b_spec = pl.BlockSpec((tm, D_B), lambda i: (i, 0))
    return pl.pallas_call(
        functools.partial(_tail_kernel, b_transposed=b_transposed),
        grid=(t // tm,),
        in_specs=[pl.BlockSpec((tm, d), lambda i: (i, 0)), pl.BlockSpec((tm, D_A), lambda i: (i, 0)),
                  b_spec, resident((D_A + D_B, d)),
                  pl.BlockSpec((1, d), const), resident((d, d_ff)), resident((d, d_ff)),
                  resident((d_ff, d)), pl.BlockSpec((1, d), const)],
        out_specs=pl.BlockSpec((tm, d), lambda i: (i, 0)),
        out_shape=jax.ShapeDtypeStruct((t, d), F32),
        compiler_params=pltpu.CompilerParams(dimension_semantics=("arbitrary",),
                                             vmem_limit_bytes=VMEM_LIMIT),
        name="tail",
    )(x, a, bmix, wo16, g2, wg16, wu16, wd16, gf)


def kernel(x_prompt, x_sample, cache_k, cache_v, page_table, norm1_g, w_in, gmlp_ln_g, gmlp_ln_b, gmlp_ws, gmlp_bs, lambda_q1, lambda_k1, lambda_q2, lambda_k2, subln_g, rel_bias, w_out, norm2_g, w_gate, w_up, w_down, final_g):
    nb, seq, d = x_prompt.shape
    db, dec, _ = x_sample.shape
    row = lambda p: p.reshape(1, -1)

    lam = jnp.stack([lambda_q1, lambda_k1, lambda_q2, lambda_k2])
    proj_params = (row(norm1_g), w_in.astype(BF16), row(gmlp_ln_g), row(gmlp_ln_b))

    xp = x_prompt.reshape(nb * seq, d)
    bs_p = jnp.repeat(gmlp_bs.T, CH_A, axis=1)
    a_p, qt_p, kf_p, vf_p, kb_p, vt_p, wo16, wg16, wu16, wd16 = _in_proj(
        xp, *proj_params, gmlp_ws, bs_p, tm=1024, cs=CHUNK, seq=seq, cast=(w_out, w_gate, w_up, w_down))
    tail_params = (wo16, row(norm2_g), wg16, wu16, wd16, row(final_g))
    bt_p = _prompt_attention(rel_bias, qt_p, kb_p.reshape(nb, seq, D_QK), vt_p, lam,
                             subln_g.reshape(DV, 1), t=512, hp=2)
    y_p = _tail(xp, a_p, bt_p, *tail_params, tm=512)

    xs = x_sample.reshape(db * dec, d)
    pos = np.arange(db * dec)
    pick = jnp.asarray(pos[:, None] % dec == np.arange(dec)[None, :], F32)
    same = jnp.asarray(pos[:, None] // dec == pos[None, :] // dec)
    ws_s = jnp.where(same, jnp.einsum('rt,gts,cs->grc', pick, gmlp_ws[:, :dec, :dec], pick,
                                      precision=lax.Precision.HIGHEST), 0.0)
    bs_s = jnp.tile(jnp.repeat(gmlp_bs[:, :dec].T, CH_A, axis=1), (db, 1))
    a_s, q_s, kf_s, vf_s, gv_s = _in_proj(xs, *proj_params, ws_s, bs_s, tm=db * dec, cs=db * dec)
    b_s = _sample_attention(page_table, rel_bias, q_s.reshape(db, dec, D_QK),
                            kf_s.reshape(db, dec * H_B, LANES), vf_s.reshape(db, dec * H_B, LANES),
                            cache_k.reshape(-1, PAGE * H_B, DV), cache_v.reshape(-1, PAGE * H_B, DV),
                            lam, row(subln_g), pages=16)
    y_s = _tail(xs, a_s, b_s.reshape(db * dec, D_B), *tail_params, tm=db * dec)

    return (y_p.reshape(nb, seq, d), y_s.reshape(db, dec, d),
            kf_p.reshape(nb, seq, H_B, 2 * DH), vf_p.reshape(nb, seq, H_B, DV),
            kf_s.reshape(db, dec, H_B, 2 * DH), vf_s.reshape(db, dec, H_B, DV),
            gv_s.reshape(db, dec, D_A))
```

```python
import functools
import math

import numpy as np
import jax
import jax.numpy as jnp
from jax import lax
from jax.experimental import pallas as pl
from jax.experimental.pallas import tpu as pltpu

F32 = jnp.float32
BF16 = jnp.bfloat16

EPS = 1e-6
LANES = 128
G_A = 4
CH_A = 128
CHUNK = 128
H_B = 4
DV = 128
DH = 64
D_A = G_A * CH_A
D_QK = H_B * 2 * DH
D_B = H_B * DV
SCALE = DH ** -0.5
LOG2E = math.log2(math.e)
LAYER_INDEX = 1
LAMBDA_INIT = 0.8 - 0.6 * math.exp(-0.3 * (LAYER_INDEX - 1))
NUM_BUCKETS = 32
MAX_EXACT = NUM_BUCKETS // 2
MAX_DISTANCE = 128
PAGE = 128
TAIL_PARTS = 2
CAST_STEPS = 16
RING = 4
ONES = 16
NEG = -1e30
VMEM_LIMIT = 56 * 1024 * 1024


def _bucket_thresholds():
    n = np.arange(0, 2 * MAX_DISTANCE)
    nf = np.maximum(n, 1).astype(np.float64)
    large = MAX_EXACT + (np.log(nf / MAX_EXACT) / math.log(MAX_DISTANCE / MAX_EXACT)
                         * (NUM_BUCKETS - MAX_EXACT)).astype(np.int32)
    bucket = np.where(n < MAX_EXACT, n, np.minimum(large, NUM_BUCKETS - 1))
    assert np.all(np.diff(bucket) >= 0) and bucket[MAX_DISTANCE - 1] == NUM_BUCKETS - 1
    return [int(n[bucket >= b].min()) for b in range(NUM_BUCKETS)]


BUCKET_START = _bucket_thresholds()
FAR = BUCKET_START[-1]
assert FAR <= LANES and FAR <= PAGE


def _rms(x, g):
    ms = jnp.mean(x * x, axis=-1, keepdims=True)
    return x * lax.rsqrt(ms + EPS) * g


def _dot(a, b):
    return jnp.dot(a, b, preferred_element_type=F32)


def _dot_nt(a, b):
    return lax.dot_general(a, b, (((1,), (1,)), ((), ())), preferred_element_type=F32)


def _bias_from_distance(n, value_of_bucket):
    val = jnp.zeros(n.shape, F32) + value_of_bucket(0)
    for b in range(1, NUM_BUCKETS):
        val = jnp.where(n >= BUCKET_START[b], value_of_bucket(b), val)
    return val


def _lambda_full(lam_ref):
    s1 = jnp.sum(lam_ref[0:1, :] * lam_ref[1:2, :], axis=-1, keepdims=True)
    s2 = jnp.sum(lam_ref[2:3, :] * lam_ref[3:4, :], axis=-1, keepdims=True)
    return jnp.exp(s1) - jnp.exp(s2) + LAMBDA_INIT


def _inproj_kernel(x_ref, g1_ref, win_ref, lng_ref, lnb_ref, ws_ref, bs_ref, *rest, cs, prompt, n_cast):
    cast_in, out_refs = rest[:n_cast], rest[n_cast:]
    cast_out, out_refs = out_refs[len(out_refs) - n_cast:], out_refs[:len(out_refs) - n_cast]
    if n_cast:
        @pl.when(pl.program_id(0) < CAST_STEPS)
        def _cast_weight_rows():
            for src, dst in zip(cast_in, cast_out):
                dst[...] = src[...].astype(BF16)

    if prompt:
        a_ref, q_ref, kf_ref, vf_ref, kb_ref, vt_ref = out_refs
    else:
        a_ref, q_ref, kf_ref, vf_ref, vn_ref = out_refs
    tm = x_ref.shape[0]
    xn0 = _rms(x_ref[0:tm // 2, :], g1_ref[...]).astype(BF16)
    v0 = _dot(xn0, win_ref[:, D_A:2 * D_A])
    xn1 = _rms(x_ref[tm // 2:tm, :], g1_ref[...]).astype(BF16)
    v = jnp.concatenate([v0, _dot(xn1, win_ref[:, D_A:2 * D_A])], axis=0)
    xn = jnp.concatenate([xn0, xn1], axis=0)
    vn16 = []
    for g in range(G_A):
        gs = slice(g * CH_A, (g + 1) * CH_A)
        vg = v[:, gs]
        mu = jnp.mean(vg, axis=-1, keepdims=True)
        d = vg - mu
        var = jnp.mean(d * d, axis=-1, keepdims=True)
        vn = d * lax.rsqrt(var + EPS) * lng_ref[:, gs] + lnb_ref[:, gs]
        if not prompt:
            vn_ref[:, gs] = vn
        vn16.append(vn.astype(BF16))

    vb = _dot(xn, win_ref[:, 2 * D_A + 2 * D_QK:])
    for hh in range(H_B):
        vf_ref[pl.ds(hh, tm, stride=H_B), :] = vb[:, hh * LANES:(hh + 1) * LANES]
    if prompt:
        vt_ref[0] = vb.T.astype(BF16)

    q = _dot(xn, win_ref[:, 2 * D_A:2 * D_A + D_QK])
    if prompt:
        q_ref[0] = (q * (SCALE * LOG2E)).T.astype(BF16)
    else:
        q_ref[...] = (q * SCALE).astype(BF16)

    u = _dot(xn, win_ref[:, 0:D_A])
    row = lax.broadcasted_iota(jnp.int32, (cs, cs), 0)
    col = lax.broadcasted_iota(jnp.int32, (cs, cs), 1)
    for g in range(G_A):
        gs = slice(g * CH_A, (g + 1) * CH_A)
        w = jnp.where(row >= col, ws_ref[g], 0.0).astype(BF16)
        for c in range(tm // cs):
            rs = slice(c * cs, (c + 1) * cs)
            s = _dot(w, vn16[g][rs, :]) + bs_ref[:, gs]
            a_ref[rs, gs] = (u[rs, gs] * s).astype(BF16)

    k = _dot(xn, win_ref[:, 2 * D_A + D_QK:2 * D_A + 2 * D_QK])
    for hh in range(H_B):
        kf_ref[pl.ds(hh, tm, stride=H_B), :] = k[:, hh * LANES:(hh + 1) * LANES]
    if prompt:
        kb_ref[...] = k.astype(BF16)


def _in_proj(x, g1, win16, lng, lnb, ws, bs_full, *, tm, cs, seq=None, cast=()):
    t, d = x.shape
    d_in = win16.shape[1]
    const = lambda i: (0, 0)
    tile = lambda w: pl.BlockSpec((tm, w), lambda i: (i, 0))
    heads = pl.BlockSpec((tm * H_B, LANES), lambda i: (i, 0))
    if seq is not None:
        tiles = seq // tm
        transposed = pl.BlockSpec((1, D_QK, tm), lambda i: (i // tiles, 0, i % tiles))
        transposed_shape = jax.ShapeDtypeStruct((t // seq, D_QK, seq), BF16)
    out_shape = [jax.ShapeDtypeStruct((t, D_A), BF16),
                 jax.ShapeDtypeStruct((t, D_QK), BF16) if seq is None else transposed_shape,
                 jax.ShapeDtypeStruct((t * H_B, LANES), F32), jax.ShapeDtypeStruct((t * H_B, LANES), F32)]
    out_specs = [tile(D_A), tile(D_QK) if seq is None else transposed, heads, heads]
    if seq is not None:
        out_shape += [jax.ShapeDtypeStruct((t, D_QK), BF16), transposed_shape]
        out_specs += [tile(D_QK), transposed]
    else:
        out_shape.append(jax.ShapeDtypeStruct((t, D_A), F32))
        out_specs.append(tile(D_A))
    cast_specs = []
    for w in cast:
        assert t // tm >= CAST_STEPS and w.shape[0] % (CAST_STEPS * 16) == 0
        cast_specs.append(pl.BlockSpec((w.shape[0] // CAST_STEPS, w.shape[1]),
                                       lambda i: (jnp.minimum(i, CAST_STEPS - 1), 0)))
        out_shape.append(jax.ShapeDtypeStruct(w.shape, BF16))
    return pl.pallas_call(
        functools.partial(_inproj_kernel, cs=cs, prompt=seq is not None, n_cast=len(cast)),
        grid=(t // tm,),
        in_specs=[tile(d), pl.BlockSpec((1, d), const), pl.BlockSpec((d, d_in), const),
                  pl.BlockSpec((1, D_A), const), pl.BlockSpec((1, D_A), const),
                  pl.BlockSpec((G_A, cs, cs), lambda i: (0, 0, 0)), pl.BlockSpec((cs, D_A), const)] + cast_specs,
        out_specs=out_specs + cast_specs,
        out_shape=out_shape,
        compiler_params=pltpu.CompilerParams(dimension_semantics=("arbitrary",),
                                             vmem_limit_bytes=VMEM_LIMIT),
        name="in_proj",
    )(x, g1, win16, lng, lnb, ws, bs_full, *cast)


def _attn_kernel(rb_ref, q_ref, k_ref, vt_ref, lam_ref, sg_ref, o_ref,
                 diag_scr, corner_scr, qzt_scr, vt1_scr, sa_scr, sb_scr, sc_scr, mxa_scr, mxb_scr, mxc_scr,
                 m_scr, acc_scr, *, t):
    b, hg = pl.program_id(0), pl.program_id(1)
    hp = vt1_scr.shape[0]
    nq = k_ref.shape[1] // t
    c = LANES

    @pl.when((b == 0) & (hg == 0))
    def _build_bias_tiles():
        d0 = lax.broadcasted_iota(jnp.int32, (t, t), 1) - lax.broadcasted_iota(jnp.int32, (t, t), 0)
        dc = c + lax.broadcasted_iota(jnp.int32, (c, c), 1) - lax.broadcasted_iota(jnp.int32, (c, c), 0)
        for hh in range(H_B):
            near = _bias_from_distance(jnp.maximum(d0, 0), lambda bk: rb_ref[bk, hh]) * LOG2E
            diag_scr[hh] = jnp.where(d0 >= 0, near, NEG)
            corner_scr[hh] = (_bias_from_distance(dc, lambda bk: rb_ref[bk, hh])
                              - rb_ref[NUM_BUCKETS - 1, hh]) * LOG2E

    def stage_values():
        for e in range(hp):
            vt1_scr[e, 0:DV, :] = vt_ref[0, e * DV:(e + 1) * DV, :]
            vt1_scr[e, DV:DV + ONES, :] = jnp.ones((ONES, vt1_scr.shape[2]), BF16)

    def stage_queries(slot, tile):
        depth = lax.broadcasted_iota(jnp.int32, (LANES, t), 0)
        for e in range(hp):
            qt = q_ref[0, e * LANES:(e + 1) * LANES, pl.ds(pl.multiple_of(tile * t, t), t)]
            zero = jnp.zeros_like(qt)
            qzt_scr[slot, e] = jnp.concatenate([jnp.where(depth < DH, qt, zero),
                                                jnp.where(depth >= DH, qt, zero)], axis=1)

    def reset_state():
        for e in range(hp):
            m_scr[e] = jnp.full(m_scr.shape[1:], -jnp.inf, F32)
            acc_scr[e] = jnp.zeros(acc_scr.shape[1:], F32)

    def issue(buf, slot, kj):
        s_scr, mx_scr = buf
        for e in range(hp):
            k = k_ref[0, pl.ds(pl.multiple_of(kj * t, t), t), e * LANES:(e + 1) * LANES]
            s = _dot(k, qzt_scr[slot, e])
            s_scr[e] = s
            mx_scr[e] = jnp.max(s, axis=0, keepdims=True)

    def accumulate(e, s, mx, shift, kj):
        m_old = m_scr[e]
        m_new = jnp.maximum(m_old, mx + shift)
        alpha = jnp.exp2(m_old - m_new)
        p = jnp.exp2((s - (m_new - shift)).astype(BF16))
        vt1 = vt1_scr[e, :, pl.ds(pl.multiple_of(kj * t, t), t)]
        acc_scr[e] = alpha * acc_scr[e] + _dot(vt1, p)
        m_scr[e] = m_new

    def consume(buf, kj, qi):
        s_scr, mx_scr = buf
        near = jnp.where(kj == qi - 1, 1.0, 0.0).astype(F32)
        for e in range(hp):
            hh = hg * hp + e
            fix = near * corner_scr[hh]
            c0 = s_scr[e, t - c:t, 0:c] + fix
            c1 = s_scr[e, t - c:t, t:t + c] + fix
            s_scr[e, t - c:t, 0:c] = c0
            s_scr[e, t - c:t, t:t + c] = c1
            mx = mx_scr[e]
            mx = jnp.concatenate([jnp.maximum(mx[:, 0:c], jnp.max(c0, axis=0, keepdims=True)), mx[:, c:t],
                                  jnp.maximum(mx[:, t:t + c], jnp.max(c1, axis=0, keepdims=True)),
                                  mx[:, t + c:]], axis=1)
            accumulate(e, s_scr[e], mx, rb_ref[NUM_BUCKETS - 1, hh] * LOG2E, kj)

    def finish_tile(buf, qi):
        lam = _lambda_full(lam_ref)
        for e in range(hp):
            bias = diag_scr[hg * hp + e]
            s = buf[0][e]
            s = jnp.concatenate([s[:, 0:t] + bias, s[:, t:2 * t] + bias], axis=1)
            accumulate(e, s, jnp.max(s, axis=0, keepdims=True), 0.0, qi)
        for e in range(hp):
            ot = acc_scr[e, 0:DV, :] / acc_scr[e, DV:DV + 1, :]
            ot = ot[:, 0:t] - lam * ot[:, t:2 * t]
            ms = jnp.mean(ot * ot, axis=0, keepdims=True)
            o_ref[0, e * DV:(e + 1) * DV, pl.ds(pl.multiple_of(qi * t, t), t)] = (
                ot * lax.rsqrt(ms + EPS) * sg_ref[...] * (1.0 - LAMBDA_INIT)).astype(o_ref.dtype)
        reset_state()

    buf_a, buf_b, buf_c = (sa_scr, mxa_scr), (sb_scr, mxb_scr), (sc_scr, mxc_scr)

    def prefetch_next_tile(qi):
        stage_queries(1, jnp.minimum(qi + 1, nq - 1))
        issue(buf_c, 1, 0)

    def tile(qi, carry):
        @pl.when(qi == 0)
        def _first_tile():
            stage_queries(1, 1)
            issue(buf_a, 1, 0)
            finish_tile(buf_c, qi)
            qzt_scr[0] = qzt_scr[1]

        @pl.when(qi == 1)
        def _second_tile():
            issue(buf_b, 0, 1)
            consume(buf_a, 0, qi)
            prefetch_next_tile(qi)
            finish_tile(buf_b, qi)
            qzt_scr[0] = qzt_scr[1]

        @pl.when(qi > 1)
        def _first_block():
            issue(buf_a, 0, 1)
            consume(buf_c, 0, qi)

        def block_pair(i, inner):
            issue(buf_b, 0, 2 * i + 2)
            consume(buf_a, 2 * i + 1, qi)
            issue(buf_a, 0, 2 * i + 3)
            consume(buf_b, 2 * i + 2, qi)
            return inner

        lax.fori_loop(0, lax.shift_right_arithmetic(qi - 1, 1), block_pair, 0)

        @pl.when((qi > 1) & (qi % 2 == 1))
        def _diagonal_in_a():
            prefetch_next_tile(qi)
            finish_tile(buf_a, qi)
            qzt_scr[0] = qzt_scr[1]

        @pl.when((qi > 0) & (qi % 2 == 0))
        def _diagonal_in_b():
            issue(buf_b, 0, qi)
            consume(buf_a, qi - 1, qi)
            prefetch_next_tile(qi)
            finish_tile(buf_b, qi)
            qzt_scr[0] = qzt_scr[1]

        return carry

    stage_queries(0, 0)
    issue(buf_c, 0, 0)
    stage_values()
    reset_state()
    lax.fori_loop(0, nq, tile, 0)


def _prompt_attention(rel_bias, qt, k, vt, lam, sg_col, *, t, hp):
    b, s, _ = k.shape
    assert s % t == 0 and s // t >= 2 and H_B % hp == 0
    const = lambda bi, hi: (0, 0)
    w = hp * LANES
    transposed = pl.BlockSpec((1, w, s), lambda bi, hi: (bi, hi, 0))
    scores = pltpu.VMEM((hp, t, 2 * t), F32)
    per_query = pltpu.VMEM((hp, 1, 2 * t), F32)
    return pl.pallas_call(
        functools.partial(_attn_kernel, t=t),
        grid=(b, H_B // hp),
        in_specs=[pl.BlockSpec(memory_space=pltpu.SMEM), transposed,
                  pl.BlockSpec((1, s, w), lambda bi, hi: (bi, 0, hi)), transposed,
                  pl.BlockSpec((4, DH), const), pl.BlockSpec((DV, 1), const)],
        out_specs=transposed,
        out_shape=jax.ShapeDtypeStruct((b, D_B, s), BF16),
        scratch_shapes=[pltpu.VMEM((H_B, t, t), F32), pltpu.VMEM((H_B, LANES, LANES), F32),
                        pltpu.VMEM((2, hp, LANES, 2 * t), BF16), pltpu.VMEM((hp, DV + ONES, s), BF16),
                        scores, scores, scores, per_query, per_query, per_query,
                        per_query, pltpu.VMEM((hp, DV + ONES, 2 * t), F32)],
        compiler_params=pltpu.CompilerParams(dimension_semantics=("arbitrary",) * 2,
                                             vmem_limit_bytes=VMEM_LIMIT),
        name="prompt_attn",
    )(rel_bias, qt, k, vt, lam, sg_col)


def _paged_kernel(pt_ref, rb_ref, q_ref, kn_ref, vn_ref, lam_ref, sg_ref, ck_hbm, cv_hbm, o_ref,
                  qh_scr, new_scr, tile_scr, far_scr, m_scr, l_scr, acc_scr, kbuf, vbuf, k_sem, v_sem,
                  *, pages, dec):
    b, st = pl.program_id(0), pl.program_id(1)
    n_st = pl.num_programs(1)
    hrows = 2 * dec
    rows = H_B * hrows
    lin = b * n_st + st
    n_lin = pl.num_programs(0) * n_st
    prows = PAGE * H_B

    def page_copies(group, slot, p):
        page = pt_ref[group * pages + p]
        dst = pl.ds(p * prows, prows)
        return (pltpu.make_async_copy(ck_hbm.at[page], kbuf.at[slot, dst], k_sem.at[slot]),
                pltpu.make_async_copy(cv_hbm.at[page], vbuf.at[slot, dst], v_sem.at[slot]))

    def start_group(group):
        slot = group % RING
        for p in range(pages):
            for cp in page_copies(group, slot, p):
                cp.start()

    @pl.when(lin == 0)
    def _prime_ring():
        for g in range(RING - 1):
            start_group(g)

    @pl.when(lin + RING - 1 < n_lin)
    def _fetch_ahead():
        start_group(lin + RING - 1)

    slot = lin % RING
    for p in range(pages):
        for cp in page_copies(lin, slot, p):
            cp.wait()

    @pl.when((b == 0) & (st == 0))
    def _build_bias_tiles():
        r = lax.broadcasted_iota(jnp.int32, (rows, PAGE), 0)
        j = lax.broadcasted_iota(jnp.int32, (rows, PAGE), 1)
        head, i = r // hrows, r % dec

        def row_value(bk):
            val = jnp.zeros((rows, PAGE), F32) + rb_ref[bk, H_B - 1]
            for hh in range(H_B - 2, -1, -1):
                val = jnp.where(head == hh, rb_ref[bk, hh], val)
            return val

        tile_scr[0] = _bias_from_distance(PAGE + i - j, row_value)
        tile_scr[1] = jnp.where(j <= i, _bias_from_distance(jnp.maximum(i - j, 0), row_value), NEG)
        far_scr[...] = row_value(NUM_BUCKETS - 1)[:, 0:1]
        new_scr[...] = jnp.zeros_like(new_scr)

    @pl.when(st == 0)
    def _start_batch():
        r = lax.broadcasted_iota(jnp.int32, (hrows, LANES), 0)
        lane = lax.broadcasted_iota(jnp.int32, (hrows, LANES), 1)
        for hh in range(H_B):
            qh = q_ref[0, :, hh * LANES:(hh + 1) * LANES].astype(F32)
            qh = jnp.concatenate([qh, qh], axis=0)
            qh_scr[hh] = jnp.where(lane // DH == r // dec, qh, 0.0).astype(BF16)
        m_scr[...] = jnp.full_like(m_scr, -jnp.inf)
        l_scr[...] = jnp.zeros_like(l_scr)
        acc_scr[...] = jnp.zeros_like(acc_scr)

    def scores(head_keys):
        return jnp.concatenate([_dot_nt(qh_scr[hh], head_keys(hh)) for hh in range(H_B)], axis=0)

    def update(s, shift, head_values):
        m_old = m_scr[...]
        m_new = jnp.maximum(m_old, jnp.max(s, axis=-1, keepdims=True) + shift)
        alpha = jnp.exp(m_old - m_new)
        p = jnp.exp(s - (m_new - shift))
        l_scr[...] = alpha * l_scr[...] + jnp.sum(p, axis=-1, keepdims=True)
        p = p.astype(BF16)
        pv = jnp.concatenate([_dot(p[hh * hrows:(hh + 1) * hrows], head_values(hh))
                              for hh in range(H_B)], axis=0)
        acc_scr[...] = alpha * acc_scr[...] + pv
        m_scr[...] = m_new

    def cached(buf, first, last):
        return lambda hh: buf[slot, pl.ds(first * prows + hh, (last - first) * PAGE, stride=H_B), :].astype(BF16)

    def fresh(which):
        return lambda hh: new_scr[which, :, hh * LANES:(hh + 1) * LANES].astype(BF16)

    @pl.when(st < n_st - 1)
    def _far_pages():
        update(scores(cached(kbuf, 0, pages)), far_scr[...], cached(vbuf, 0, pages))

    @pl.when(st == n_st - 1)
    def _last_pages_and_new_rows():
        if pages > 1:
            update(scores(cached(kbuf, 0, pages - 1)), far_scr[...], cached(vbuf, 0, pages - 1))
        update(scores(cached(kbuf, pages - 1, pages)) + tile_scr[0], 0.0, cached(vbuf, pages - 1, pages))
        for hh in range(H_B):
            new_scr[0, 0:dec, hh * LANES:(hh + 1) * LANES] = kn_ref[0, pl.ds(hh, dec, stride=H_B), :]
            new_scr[1, 0:dec, hh * LANES:(hh + 1) * LANES] = vn_ref[0, pl.ds(hh, dec, stride=H_B), :]
        update(scores(fresh(0)) + tile_scr[1], 0.0, fresh(1))

        o = acc_scr[...] / l_scr[...]
        lam = _lambda_full(lam_ref)
        for hh in range(H_B):
            r0 = hh * hrows
            oh = o[r0:r0 + dec, :] - lam * o[r0 + dec:r0 + hrows, :]
            o_ref[0, :, hh * DV:(hh + 1) * DV] = (_rms(oh, sg_ref[...])
                                                  * (1.0 - LAMBDA_INIT)).astype(o_ref.dtype)


def _sample_attention(page_table, rel_bias, q, k_new, v_new, cache_k, cache_v, lam, sg, *, pages):
    nb, dec, _ = q.shape
    n_pages = page_table.shape[1]
    n_st = n_pages // pages
    rows = 2 * H_B * dec
    const2 = lambda bi, si, pt: (0, 0)
    per_batch = pl.BlockSpec((1, dec, D_QK), lambda bi, si, pt: (bi, 0, 0))
    new_rows = pl.BlockSpec((1, dec * H_B, LANES), lambda bi, si, pt: (bi, 0, 0))
    assert n_pages % pages == 0 and nb * n_st >= RING
    ring = pltpu.VMEM((RING, pages * PAGE * H_B, LANES), F32)

    grid_spec = pltpu.PrefetchScalarGridSpec(
        num_scalar_prefetch=1,
        grid=(nb, n_st),
        in_specs=[pl.BlockSpec(memory_space=pltpu.SMEM), per_batch, new_rows, new_rows,
                  pl.BlockSpec((4, DH), const2), pl.BlockSpec((1, DV), const2),
                  pl.BlockSpec(memory_space=pl.ANY), pl.BlockSpec(memory_space=pl.ANY)],
        out_specs=per_batch,
        scratch_shapes=[pltpu.VMEM((H_B, 2 * dec, LANES), BF16), pltpu.VMEM((2, PAGE, D_QK), F32),
                        pltpu.VMEM((2, rows, PAGE), F32), pltpu.VMEM((rows, 1), F32),
                        pltpu.VMEM((rows, 1), F32), pltpu.VMEM((rows, 1), F32),
                        pltpu.VMEM((rows, DV), F32), ring, ring,
                        pltpu.SemaphoreType.DMA((RING,)), pltpu.SemaphoreType.DMA((RING,))],
    )
    return pl.pallas_call(
        functools.partial(_paged_kernel, pages=pages, dec=dec),
        grid_spec=grid_spec,
        out_shape=jax.ShapeDtypeStruct((nb, dec, D_B), BF16),
        compiler_params=pltpu.CompilerParams(dimension_semantics=("arbitrary", "arbitrary"),
                                             vmem_limit_bytes=VMEM_LIMIT),
        name="paged_attn",
    )(page_table.reshape(-1), rel_bias, q, k_new, v_new, lam, sg, cache_k, cache_v)


def _tail_kernel(x_ref, a_ref, b_ref, wo_ref, g2_ref, wg_ref, wu_ref, wd_ref, gf_ref, y_ref, *, b_transposed):
    tm = x_ref.shape[0]
    halves = [slice(p * tm // TAIL_PARTS, (p + 1) * tm // TAIL_PARTS) for p in range(TAIL_PARTS)]
    hs = []
    for rs in halves:
        if b_transposed:
            bw = lax.dot_general(b_ref[0, :, rs], wo_ref[D_A:, :], (((0,), (0,)), ((), ())),
                                 preferred_element_type=F32)
        else:
            bw = _dot(b_ref[rs, :], wo_ref[D_A:, :])
        hs.append(x_ref[rs, :] + _dot(a_ref[rs, :], wo_ref[0:D_A, :]) + bw)
    fs = []
    for h in hs:
        hn = _rms(h, g2_ref[...]).astype(BF16)
        gate = _dot(hn, wg_ref[...])
        fs.append((gate * (1.0 / (1.0 + jnp.exp(-gate))) * _dot(hn, wu_ref[...])).astype(BF16))
    for rs, h, f in zip(halves, hs, fs):
        y_ref[rs, :] = _rms(h + _dot(f, wd_ref[...]), gf_ref[...])


def _tail(x, a, bmix, wo16, g2, wg16, wu16, wd16, gf, *, tm):
    t, d = x.shape
    d_ff = wg16.shape[1]
    const = lambda i: (0, 0)
    resident = lambda shape: pl.BlockSpec(shape, const, pipeline_mode=pl.Buffered(1))
    b_transposed = bmix.ndim == 3
    if b_transposed:
        tiles = bmix.shape[2] // tm
        b_spec = pl.BlockSpec((1, D_B, tm), lambda i: (i // tiles, 0, i % tiles))
    else:
        b_spec = pl.BlockSpec((tm, D_B), lambda i: (i, 0))
    return pl.pallas_call(
        functools.partial(_tail_kernel, b_transposed=b_transposed),
        grid=(t // tm,),
        in_specs=[pl.BlockSpec((tm, d), lambda i: (i, 0)), pl.BlockSpec((tm, D_A), lambda i: (i, 0)),
                  b_spec, resident((D_A + D_B, d)),
                  pl.BlockSpec((1, d), const), resident((d, d_ff)), resident((d, d_ff)),
                  resident((d_ff, d)), pl.BlockSpec((1, d), const)],
        out_specs=pl.BlockSpec((tm, d), lambda i: (i, 0)),
        out_shape=jax.ShapeDtypeStruct((t, d), F32),
        compiler_params=pltpu.CompilerParams(dimension_semantics=("arbitrary",),
                                             vmem_limit_bytes=VMEM_LIMIT),
        name="tail",
    )(x, a, bmix, wo16, g2, wg16, wu16, wd16, gf)


def kernel(x_prompt, x_sample, cache_k, cache_v, page_table, norm1_g, w_in, gmlp_ln_g, gmlp_ln_b, gmlp_ws, gmlp_bs, lambda_q1, lambda_k1, lambda_q2, lambda_k2, subln_g, rel_bias, w_out, norm2_g, w_gate, w_up, w_down, final_g):
    nb, seq, d = x_prompt.shape
    db, dec, _ = x_sample.shape
    row = lambda p: p.reshape(1, -1)

    lam = jnp.stack([lambda_q1, lambda_k1, lambda_q2, lambda_k2])
    proj_params = (row(norm1_g), w_in.astype(BF16), row(gmlp_ln_g), row(gmlp_ln_b))

    xp = x_prompt.reshape(nb * seq, d)
    bs_p = jnp.repeat(gmlp_bs.T, CH_A, axis=1)
    a_p, qt_p, kf_p, vf_p, kb_p, vt_p, wo16, wg16, wu16, wd16 = _in_proj(
        xp, *proj_params, gmlp_ws, bs_p, tm=1024, cs=CHUNK, seq=seq, cast=(w_out, w_gate, w_up, w_down))
    tail_params = (wo16, row(norm2_g), wg16, wu16, wd16, row(final_g))
    bt_p = _prompt_attention(rel_bias, qt_p, kb_p.reshape(nb, seq, D_QK), vt_p, lam,
                             subln_g.reshape(DV, 1), t=512, hp=2)
    y_p = _tail(xp, a_p, bt_p, *tail_params, tm=512)

    xs = x_sample.reshape(db * dec, d)
    pos = np.arange(db * dec)
    pick = jnp.asarray(pos[:, None] % dec == np.arange(dec)[None, :], F32)
    same = jnp.asarray(pos[:, None] // dec == pos[None, :] // dec)
    ws_s = jnp.where(same, jnp.einsum('rt,gts,cs->grc', pick, gmlp_ws[:, :dec, :dec], pick,
                                      precision=lax.Precision.HIGHEST), 0.0)
    bs_s = jnp.tile(jnp.repeat(gmlp_bs[:, :dec].T, CH_A, axis=1), (db, 1))
    a_s, q_s, kf_s, vf_s, gv_s = _in_proj(xs, *proj_params, ws_s, bs_s, tm=db * dec, cs=db * dec)
    b_s = _sample_attention(page_table, rel_bias, q_s.reshape(db, dec, D_QK),
                            kf_s.reshape(db, dec * H_B, LANES), vf_s.reshape(db, dec * H_B, LANES),
                            cache_k.reshape(-1, PAGE * H_B, DV), cache_v.reshape(-1, PAGE * H_B, DV),
                            lam, row(subln_g), pages=16)
    y_s = _tail(xs, a_s, b_s.reshape(db * dec, D_B), *tail_params, tm=db * dec)

    return (y_p.reshape(nb, seq, d), y_s.reshape(db, dec, d),
            kf_p.reshape(nb, seq, H_B, 2 * DH), vf_p.reshape(nb, seq, H_B, DV),
            kf_s.reshape(db, dec, H_B, 2 * DH), vf_s.reshape(db, dec, H_B, DV),
            gv_s.reshape(db, dec, D_A))
```

```python
import functools
import math

import numpy as np
import jax
import jax.numpy as jnp
from jax import lax
from jax.experimental import pallas as pl
from jax.experimental.pallas import tpu as pltpu

F32 = jnp.float32
BF16 = jnp.bfloat16

EPS = 1e-6
LANES = 128
G_A = 4
CH_A = 128
CHUNK = 128
H_B = 4
DV = 128
DH = 64
D_A = G_A * CH_A
D_QK = H_B * 2 * DH
D_B = H_B * DV
SCALE = DH ** -0.5
LOG2E = math.log2(math.e)
LAYER_INDEX = 1
LAMBDA_INIT = 0.8 - 0.6 * math.exp(-0.3 * (LAYER_INDEX - 1))
NUM_BUCKETS = 32
MAX_EXACT = NUM_BUCKETS // 2
MAX_DISTANCE = 128
PAGE = 128
TAIL_PARTS = 2
CAST_STEPS = 16
RING = 3
ONES = 16
NEG = -1e30
VMEM_LIMIT = 56 * 1024 * 1024


def _bucket_thresholds():
    n = np.arange(0, 2 * MAX_DISTANCE)
    nf = np.maximum(n, 1).astype(np.float64)
    large = MAX_EXACT + (np.log(nf / MAX_EXACT) / math.log(MAX_DISTANCE / MAX_EXACT)
                         * (NUM_BUCKETS - MAX_EXACT)).astype(np.int32)
    bucket = np.where(n < MAX_EXACT, n, np.minimum(large, NUM_BUCKETS - 1))
    assert np.all(np.diff(bucket) >= 0) and bucket[MAX_DISTANCE - 1] == NUM_BUCKETS - 1
    return [int(n[bucket >= b].min()) for b in range(NUM_BUCKETS)]


BUCKET_START = _bucket_thresholds()
FAR = BUCKET_START[-1]
assert FAR <= LANES and FAR <= PAGE


def _rms(x, g):
    ms = jnp.mean(x * x, axis=-1, keepdims=True)
    return x * lax.rsqrt(ms + EPS) * g


def _dot(a, b):
    return jnp.dot(a, b, preferred_element_type=F32)


def _dot_nt(a, b):
    return lax.dot_general(a, b, (((1,), (1,)), ((), ())), preferred_element_type=F32)


def _bias_from_distance(n, value_of_bucket):
    val = jnp.zeros(n.shape, F32) + value_of_bucket(0)
    for b in range(1, NUM_BUCKETS):
        val = jnp.where(n >= BUCKET_START[b], value_of_bucket(b), val)
    return val


def _lambda_full(lam_ref):
    s1 = jnp.sum(lam_ref[0:1, :] * lam_ref[1:2, :], axis=-1, keepdims=True)
    s2 = jnp.sum(lam_ref[2:3, :] * lam_ref[3:4, :], axis=-1, keepdims=True)
    return jnp.exp(s1) - jnp.exp(s2) + LAMBDA_INIT


def _inproj_kernel(x_ref, g1_ref, win_ref, lng_ref, lnb_ref, ws_ref, bs_ref, *rest, cs, prompt, n_cast):
    cast_in, out_refs = rest[:n_cast], rest[n_cast:]
    cast_out, out_refs = out_refs[len(out_refs) - n_cast:], out_refs[:len(out_refs) - n_cast]
    if n_cast:
        @pl.when(pl.program_id(0) < CAST_STEPS)
        def _cast_weight_rows():
            for src, dst in zip(cast_in, cast_out):
                dst[...] = src[...].astype(BF16)

    if prompt:
        a_ref, q_ref, kf_ref, vf_ref, kb_ref, vt_ref = out_refs
    else:
        a_ref, q_ref, kf_ref, vf_ref, vn_ref = out_refs
    tm = x_ref.shape[0]
    xn0 = _rms(x_ref[0:tm // 2, :], g1_ref[...]).astype(BF16)
    v0 = _dot(xn0, win_ref[:, D_A:2 * D_A])
    xn1 = _rms(x_ref[tm // 2:tm, :], g1_ref[...]).astype(BF16)
    v = jnp.concatenate([v0, _dot(xn1, win_ref[:, D_A:2 * D_A])], axis=0)
    xn = jnp.concatenate([xn0, xn1], axis=0)
    vn16 = []
    for g in range(G_A):
        gs = slice(g * CH_A, (g + 1) * CH_A)
        vg = v[:, gs]
        mu = jnp.mean(vg, axis=-1, keepdims=True)
        d = vg - mu
        var = jnp.mean(d * d, axis=-1, keepdims=True)
        vn = d * lax.rsqrt(var + EPS) * lng_ref[:, gs] + lnb_ref[:, gs]
        if not prompt:
            vn_ref[:, gs] = vn
        vn16.append(vn.astype(BF16))

    vb = _dot(xn, win_ref[:, 2 * D_A + 2 * D_QK:])
    for hh in range(H_B):
        vf_ref[pl.ds(hh, tm, stride=H_B), :] = vb[:, hh * LANES:(hh + 1) * LANES]
    if prompt:
        vt_ref[0] = vb.T.astype(BF16)

    q = _dot(xn, win_ref[:, 2 * D_A:2 * D_A + D_QK])
    if prompt:
        q_ref[0] = (q * (SCALE * LOG2E)).T.astype(BF16)
    else:
        q_ref[...] = (q * SCALE).astype(BF16)

    u = _dot(xn, win_ref[:, 0:D_A])
    row = lax.broadcasted_iota(jnp.int32, (cs, cs), 0)
    col = lax.broadcasted_iota(jnp.int32, (cs, cs), 1)
    for g in range(G_A):
        gs = slice(g * CH_A, (g + 1) * CH_A)
        w = jnp.where(row >= col, ws_ref[g], 0.0).astype(BF16)
        for c in range(tm // cs):
            rs = slice(c * cs, (c + 1) * cs)
            s = _dot(w, vn16[g][rs, :]) + bs_ref[:, gs]
            a_ref[rs, gs] = (u[rs, gs] * s).astype(BF16)

    k = _dot(xn, win_ref[:, 2 * D_A + D_QK:2 * D_A + 2 * D_QK])
    for hh in range(H_B):
        kf_ref[pl.ds(hh, tm, stride=H_B), :] = k[:, hh * LANES:(hh + 1) * LANES]
    if prompt:
        kb_ref[...] = k.astype(BF16)


def _in_proj(x, g1, win16, lng, lnb, ws, bs_full, *, tm, cs, seq=None, cast=()):
    t, d = x.shape
    d_in = win16.shape[1]
    const = lambda i: (0, 0)
    tile = lambda w: pl.BlockSpec((tm, w), lambda i: (i, 0))
    heads = pl.BlockSpec((tm * H_B, LANES), lambda i: (i, 0))
    if seq is not None:
        tiles = seq // tm
        transposed = pl.BlockSpec((1, D_QK, tm), lambda i: (i // tiles, 0, i % tiles))
        transposed_shape = jax.ShapeDtypeStruct((t // seq, D_QK, seq), BF16)
    out_shape = [jax.ShapeDtypeStruct((t, D_A), BF16),
                 jax.ShapeDtypeStruct((t, D_QK), BF16) if seq is None else transposed_shape,
                 jax.ShapeDtypeStruct((t * H_B, LANES), F32), jax.ShapeDtypeStruct((t * H_B, LANES), F32)]
    out_specs = [tile(D_A), tile(D_QK) if seq is None else transposed, heads, heads]
    if seq is not None:
        out_shape += [jax.ShapeDtypeStruct((t, D_QK), BF16), transposed_shape]
        out_specs += [tile(D_QK), transposed]
    else:
        out_shape.append(jax.ShapeDtypeStruct((t, D_A), F32))
        out_specs.append(tile(D_A))
    cast_specs = []
    for w in cast:
        assert t // tm >= CAST_STEPS and w.shape[0] % (CAST_STEPS * 16) == 0
        cast_specs.append(pl.BlockSpec((w.shape[0] // CAST_STEPS, w.shape[1]),
                                       lambda i: (jnp.minimum(i, CAST_STEPS - 1), 0)))
        out_shape.append(jax.ShapeDtypeStruct(w.shape, BF16))
    return pl.pallas_call(
        functools.partial(_inproj_kernel, cs=cs, prompt=seq is not None, n_cast=len(cast)),
        grid=(t // tm,),
        in_specs=[tile(d), pl.BlockSpec((1, d), const), pl.BlockSpec((d, d_in), const),
                  pl.BlockSpec((1, D_A), const), pl.BlockSpec((1, D_A), const),
                  pl.BlockSpec((G_A, cs, cs), lambda i: (0, 0, 0)), pl.BlockSpec((cs, D_A), const)] + cast_specs,
        out_specs=out_specs + cast_specs,
        out_shape=out_shape,
        compiler_params=pltpu.CompilerParams(dimension_semantics=("arbitrary",),
                                             vmem_limit_bytes=VMEM_LIMIT),
        name="in_proj",
    )(x, g1, win16, lng, lnb, ws, bs_full, *cast)


def _attn_kernel(rb_ref, q_ref, k_ref, vt_ref, lam_ref, sg_ref, o_ref,
                 diag_scr, corner_scr, qzt_scr, vt1_scr, sa_scr, sb_scr, sc_scr, mxa_scr, mxb_scr, mxc_scr,
                 m_scr, acc_scr, *, t):
    b, hg = pl.program_id(0), pl.program_id(1)
    hp = vt1_scr.shape[0]
    nq = k_ref.shape[1] // t
    c = LANES

    @pl.when((b == 0) & (hg == 0))
    def _build_bias_tiles():
        d0 = lax.broadcasted_iota(jnp.int32, (t, t), 1) - lax.broadcasted_iota(jnp.int32, (t, t), 0)
        dc = c + lax.broadcasted_iota(jnp.int32, (c, c), 1) - lax.broadcasted_iota(jnp.int32, (c, c), 0)
        for hh in range(H_B):
            near = _bias_from_distance(jnp.maximum(d0, 0), lambda bk: rb_ref[bk, hh]) * LOG2E
            diag_scr[hh] = jnp.where(d0 >= 0, near, NEG)
            corner_scr[hh] = (_bias_from_distance(dc, lambda bk: rb_ref[bk, hh])
                              - rb_ref[NUM_BUCKETS - 1, hh]) * LOG2E

    def stage_values():
        for e in range(hp):
            vt1_scr[e, 0:DV, :] = vt_ref[0, e * DV:(e + 1) * DV, :]
            vt1_scr[e, DV:DV + ONES, :] = jnp.ones((ONES, vt1_scr.shape[2]), BF16)

    def stage_queries(slot, tile):
        depth = lax.broadcasted_iota(jnp.int32, (LANES, t), 0)
        for e in range(hp):
            qt = q_ref[0, e * LANES:(e + 1) * LANES, pl.ds(pl.multiple_of(tile * t, t), t)]
            zero = jnp.zeros_like(qt)
            qzt_scr[slot, e] = jnp.concatenate([jnp.where(depth < DH, qt, zero),
                                                jnp.where(depth >= DH, qt, zero)], axis=1)

    def reset_state():
        for e in range(hp):
            m_scr[e] = jnp.full(m_scr.shape[1:], -jnp.inf, F32)
            acc_scr[e] = jnp.zeros(acc_scr.shape[1:], F32)

    def issue(buf, slot, kj):
        s_scr, mx_scr = buf
        for e in range(hp):
            k = k_ref[0, pl.ds(pl.multiple_of(kj * t, t), t), e * LANES:(e + 1) * LANES]
            s = _dot(k, qzt_scr[slot, e])
            s_scr[e] = s
            mx_scr[e] = jnp.max(s, axis=0, keepdims=True)

    def accumulate(e, s, mx, shift, kj):
        m_old = m_scr[e]
        m_new = jnp.maximum(m_old, mx + shift)
        alpha = jnp.exp2(m_old - m_new)
        ref = m_new - shift
        vt1 = vt1_scr[e, :, pl.ds(pl.multiple_of(kj * t, t), t)]
        for half in (slice(0, t), slice(t, 2 * t)):
            p = jnp.exp2((s[:, half] - ref[:, half]).astype(BF16))
            acc_scr[e, :, half] = alpha[:, half] * acc_scr[e, :, half] + _dot(vt1, p)
        m_scr[e] = m_new

    def consume(buf, kj, qi):
        s_scr, mx_scr = buf
        near = jnp.where(kj == qi - 1, 1.0, 0.0).astype(F32)
        for e in range(hp):
            hh = hg * hp + e
            fix = near * corner_scr[hh]
            c0 = s_scr[e, t - c:t, 0:c] + fix
            c1 = s_scr[e, t - c:t, t:t + c] + fix
            s_scr[e, t - c:t, 0:c] = c0
            s_scr[e, t - c:t, t:t + c] = c1
            mx = mx_scr[e]
            mx = jnp.concatenate([jnp.maximum(mx[:, 0:c], jnp.max(c0, axis=0, keepdims=True)), mx[:, c:t],
                                  jnp.maximum(mx[:, t:t + c], jnp.max(c1, axis=0, keepdims=True)),
                                  mx[:, t + c:]], axis=1)
            accumulate(e, s_scr[e], mx, rb_ref[NUM_BUCKETS - 1, hh] * LOG2E, kj)

    def finish_tile(buf, qi):
        lam = _lambda_full(lam_ref)
        for e in range(hp):
            bias = diag_scr[hg * hp + e]
            s = buf[0][e]
            s = jnp.concatenate([s[:, 0:t] + bias, s[:, t:2 * t] + bias], axis=1)
            accumulate(e, s, jnp.max(s, axis=0, keepdims=True), 0.0, qi)
        for e in range(hp):
            ot = acc_scr[e, 0:DV, :] / acc_scr[e, DV:DV + 1, :]
            ot = ot[:, 0:t] - lam * ot[:, t:2 * t]
            ms = jnp.mean(ot * ot, axis=0, keepdims=True)
            o_ref[0, e * DV:(e + 1) * DV, pl.ds(pl.multiple_of(qi * t, t), t)] = (
                ot * lax.rsqrt(ms + EPS) * sg_ref[...] * (1.0 - LAMBDA_INIT)).astype(o_ref.dtype)
        reset_state()

    buf_a, buf_b, buf_c = (sa_scr, mxa_scr), (sb_scr, mxb_scr), (sc_scr, mxc_scr)

    def prefetch_next_tile(qi):
        stage_queries(1, jnp.minimum(qi + 1, nq - 1))
        issue(buf_c, 1, 0)

    def tile(qi, carry):
        @pl.when(qi == 0)
        def _first_tile():
            finish_tile(buf_c, qi)
            stage_queries(0, 1)
            issue(buf_c, 0, 0)

        @pl.when(qi > 0)
        def _first_block():
            issue(buf_a, 0, 1)
            consume(buf_c, 0, qi)

        def block_pair(i, inner):
            issue(buf_b, 0, 2 * i + 2)
            consume(buf_a, 2 * i + 1, qi)
            issue(buf_a, 0, 2 * i + 3)
            consume(buf_b, 2 * i + 2, qi)
            return inner

        lax.fori_loop(0, lax.shift_right_arithmetic(qi - 1, 1), block_pair, 0)

        @pl.when((qi > 0) & (qi % 2 == 1))
        def _diagonal_in_a():
            prefetch_next_tile(qi)
            finish_tile(buf_a, qi)
            qzt_scr[0] = qzt_scr[1]

        @pl.when((qi > 0) & (qi % 2 == 0))
        def _diagonal_in_b():
            issue(buf_b, 0, qi)
            consume(buf_a, qi - 1, qi)
            prefetch_next_tile(qi)
            finish_tile(buf_b, qi)
            qzt_scr[0] = qzt_scr[1]

        return carry

    stage_queries(0, 0)
    issue(buf_c, 0, 0)
    stage_values()
    reset_state()
    lax.fori_loop(0, nq, tile, 0)


def _prompt_attention(rel_bias, qt, k, vt, lam, sg_col, *, t, hp):
    b, s, _ = k.shape
    assert s % t == 0 and s // t >= 2 and H_B % hp == 0
    const = lambda bi, hi: (0, 0)
    w = hp * LANES
    transposed = pl.BlockSpec((1, w, s), lambda bi, hi: (bi, hi, 0))
    scores = pltpu.VMEM((hp, t, 2 * t), F32)
    per_query = pltpu.VMEM((hp, 1, 2 * t), F32)
    return pl.pallas_call(
        functools.partial(_attn_kernel, t=t),
        grid=(b, H_B // hp),
        in_specs=[pl.BlockSpec(memory_space=pltpu.SMEM), transposed,
                  pl.BlockSpec((1, s, w), lambda bi, hi: (bi, 0, hi)), transposed,
                  pl.BlockSpec((4, DH), const), pl.BlockSpec((DV, 1), const)],
        out_specs=transposed,
        out_shape=jax.ShapeDtypeStruct((b, D_B, s), BF16),
        scratch_shapes=[pltpu.VMEM((H_B, t, t), F32), pltpu.VMEM((H_B, LANES, LANES), F32),
                        pltpu.VMEM((2, hp, LANES, 2 * t), BF16), pltpu.VMEM((hp, DV + ONES, s), BF16),
                        scores, scores, scores, per_query, per_query, per_query,
                        per_query, pltpu.VMEM((hp, DV + ONES, 2 * t), F32)],
        compiler_params=pltpu.CompilerParams(dimension_semantics=("arbitrary",) * 2,
                                             vmem_limit_bytes=VMEM_LIMIT),
        name="prompt_attn",
    )(rel_bias, qt, k, vt, lam, sg_col)


def _paged_kernel(pt_ref, rb_ref, q_ref, kn_ref, vn_ref, lam_ref, sg_ref, ck_hbm, cv_hbm, o_ref,
                  qh_scr, new_scr, tile_scr, far_scr, m_scr, l_scr, acc_scr, kbuf, vbuf, k_sem, v_sem,
                  *, pages, dec):
    b, st = pl.program_id(0), pl.program_id(1)
    n_st = pl.num_programs(1)
    hrows = 2 * dec
    rows = H_B * hrows
    lin = b * n_st + st
    n_lin = pl.num_programs(0) * n_st
    prows = PAGE * H_B

    def page_copies(group, slot, p):
        page = pt_ref[group * pages + p]
        dst = pl.ds(p * prows, prows)
        return (pltpu.make_async_copy(ck_hbm.at[page], kbuf.at[slot, dst], k_sem.at[slot]),
                pltpu.make_async_copy(cv_hbm.at[page], vbuf.at[slot, dst], v_sem.at[slot]))

    def start_group(group):
        slot = group % RING
        for p in range(pages):
            for cp in page_copies(group, slot, p):
                cp.start()

    @pl.when(lin == 0)
    def _prime_ring():
        for g in range(RING - 1):
            start_group(g)

    slot = lin % RING
    for p in range(pages):
        for cp in page_copies(lin, slot, p):
            cp.wait()

    @pl.when((b == 0) & (st == 0))
    def _build_bias_tiles():
        r = lax.broadcasted_iota(jnp.int32, (rows, PAGE), 0)
        j = lax.broadcasted_iota(jnp.int32, (rows, PAGE), 1)
        head, i = r // hrows, r % dec

        def row_value(bk):
            val = jnp.zeros((rows, PAGE), F32) + rb_ref[bk, H_B - 1]
            for hh in range(H_B - 2, -1, -1):
                val = jnp.where(head == hh, rb_ref[bk, hh], val)
            return val

        tile_scr[0] = _bias_from_distance(PAGE + i - j, row_value)
        tile_scr[1] = jnp.where(j <= i, _bias_from_distance(jnp.maximum(i - j, 0), row_value), NEG)
        far_scr[...] = row_value(NUM_BUCKETS - 1)[:, 0:1]
        new_scr[...] = jnp.zeros_like(new_scr)

    @pl.when(st == 0)
    def _start_batch():
        r = lax.broadcasted_iota(jnp.int32, (hrows, LANES), 0)
        lane = lax.broadcasted_iota(jnp.int32, (hrows, LANES), 1)
        for hh in range(H_B):
            qh = q_ref[0, :, hh * LANES:(hh + 1) * LANES].astype(F32)
            qh = jnp.concatenate([qh, qh], axis=0)
            qh_scr[hh] = jnp.where(lane // DH == r // dec, qh, 0.0).astype(BF16)
        m_scr[...] = jnp.full_like(m_scr, -jnp.inf)
        l_scr[...] = jnp.zeros_like(l_scr)
        acc_scr[...] = jnp.zeros_like(acc_scr)

    def scores(head_keys):
        return jnp.concatenate([_dot_nt(qh_scr[hh], head_keys(hh)) for hh in range(H_B)], axis=0)

    def update(s, shift, head_values):
        m_old = m_scr[...]
        m_new = jnp.maximum(m_old, jnp.max(s, axis=-1, keepdims=True) + shift)
        alpha = jnp.exp(m_old - m_new)
        p = jnp.exp(s - (m_new - shift))
        l_scr[...] = alpha * l_scr[...] + jnp.sum(p, axis=-1, keepdims=True)
        p = p.astype(BF16)
        pv = jnp.concatenate([_dot(p[hh * hrows:(hh + 1) * hrows], head_values(hh))
                              for hh in range(H_B)], axis=0)
        acc_scr[...] = alpha * acc_scr[...] + pv
        m_scr[...] = m_new

    def cached(buf, first, last):
        return lambda hh: buf[slot, pl.ds(first * prows + hh, (last - first) * PAGE, stride=H_B), :].astype(BF16)

    def fresh(which):
        return lambda hh: new_scr[which, :, hh * LANES:(hh + 1) * LANES].astype(BF16)

    @pl.when(st < n_st - 1)
    def _far_pages():
        update(scores(cached(kbuf, 0, pages)), far_scr[...], cached(vbuf, 0, pages))

    @pl.when(st == n_st - 1)
    def _last_pages_and_new_rows():
        if pages > 1:
            update(scores(cached(kbuf, 0, pages - 1)), far_scr[...], cached(vbuf, 0, pages - 1))
        update(scores(cached(kbuf, pages - 1, pages)) + tile_scr[0], 0.0, cached(vbuf, pages - 1, pages))
        for hh in range(H_B):
            new_scr[0, 0:dec, hh * LANES:(hh + 1) * LANES] = kn_ref[0, pl.ds(hh, dec, stride=H_B), :]
            new_scr[1, 0:dec, hh * LANES:(hh + 1) * LANES] = vn_ref[0, pl.ds(hh, dec, stride=H_B), :]
        update(scores(fresh(0)) + tile_scr[1], 0.0, fresh(1))

        o = acc_scr[...] / l_scr[...]
        lam = _lambda_full(lam_ref)
        for hh in range(H_B):
            r0 = hh * hrows
            oh = o[r0:r0 + dec, :] - lam * o[r0 + dec:r0 + hrows, :]
            o_ref[0, :, hh * DV:(hh + 1) * DV] = (_rms(oh, sg_ref[...])
                                                  * (1.0 - LAMBDA_INIT)).astype(o_ref.dtype)

    @pl.when(lin + RING - 1 < n_lin)
    def _fetch_ahead():
        start_group(lin + RING - 1)


def _sample_attention(page_table, rel_bias, q, k_new, v_new, cache_k, cache_v, lam, sg, *, pages):
    nb, dec, _ = q.shape
    n_pages = page_table.shape[1]
    n_st = n_pages // pages
    rows = 2 * H_B * dec
    const2 = lambda bi, si, pt: (0, 0)
    per_batch = pl.BlockSpec((1, dec, D_QK), lambda bi, si, pt: (bi, 0, 0))
    new_rows = pl.BlockSpec((1, dec * H_B, LANES), lambda bi, si, pt: (bi, 0, 0))
    assert n_pages % pages == 0 and nb * n_st >= RING
    ring = pltpu.VMEM((RING, pages * PAGE * H_B, LANES), F32)

    grid_spec = pltpu.PrefetchScalarGridSpec(
        num_scalar_prefetch=1,
        grid=(nb, n_st),
        in_specs=[pl.BlockSpec(memory_space=pltpu.SMEM), per_batch, new_rows, new_rows,
                  pl.BlockSpec((4, DH), const2), pl.BlockSpec((1, DV), const2),
                  pl.BlockSpec(memory_space=pl.ANY), pl.BlockSpec(memory_space=pl.ANY)],
        out_specs=per_batch,
        scratch_shapes=[pltpu.VMEM((H_B, 2 * dec, LANES), BF16), pltpu.VMEM((2, PAGE, D_QK), F32),
                        pltpu.VMEM((2, rows, PAGE), F32), pltpu.VMEM((rows, 1), F32),
                        pltpu.VMEM((rows, 1), F32), pltpu.VMEM((rows, 1), F32),
                        pltpu.VMEM((rows, DV), F32), ring, ring,
                        pltpu.SemaphoreType.DMA((RING,)), pltpu.SemaphoreType.DMA((RING,))],
    )
    return pl.pallas_call(
        functools.partial(_paged_kernel, pages=pages, dec=dec),
        grid_spec=grid_spec,
        out_shape=jax.ShapeDtypeStruct((nb, dec, D_B), BF16),
        compiler_params=pltpu.CompilerParams(dimension_semantics=("arbitrary", "arbitrary"),
                                             vmem_limit_bytes=VMEM_LIMIT),
        name="paged_attn",
    )(page_table.reshape(-1), rel_bias, q, k_new, v_new, lam, sg, cache_k, cache_v)


def _tail_kernel(x_ref, a_ref, b_ref, wo_ref, g2_ref, wg_ref, wu_ref, wd_ref, gf_ref, y_ref, *, b_transposed):
    tm = x_ref.shape[0]
    halves = [slice(p * tm // TAIL_PARTS, (p + 1) * tm // TAIL_PARTS) for p in range(TAIL_PARTS)]
    hs = []
    for rs in halves:
        if b_transposed:
            bw = lax.dot_general(b_ref[0, :, rs], wo_ref[D_A:, :], (((0,), (0,)), ((), ())),
                                 preferred_element_type=F32)
        else:
            bw = _dot(b_ref[rs, :], wo_ref[D_A:, :])
        hs.append(x_ref[rs, :] + _dot(a_ref[rs, :], wo_ref[0:D_A, :]) + bw)
    fs = []
    for h in hs:
        hn = _rms(h, g2_ref[...]).astype(BF16)
        gate = _dot(hn, wg_ref[...])
        fs.append((gate * (1.0 / (1.0 + jnp.exp(-gate))) * _dot(hn, wu_ref[...])).astype(BF16))
    for rs, h, f in zip(halves, hs, fs):
        y_ref[rs, :] = _rms(h + _dot(f, wd_ref[...]), gf_ref[...])


def _tail(x, a, bmix, wo16, g2, wg16, wu16, wd16, gf, *, tm):
    t, d = x.shape
    d_ff = wg16.shape[1]
    const = lambda i: (0, 0)
    resident = lambda shape: pl.BlockSpec(shape, const, pipeline_mode=pl.Buffered(1))
    b_transposed = bmix.ndim == 3
    if b_transposed:
        tiles = bmix.shape[2] // tm
        b_spec = pl.BlockSpec((1, D_B, tm), lambda i: (i // tiles, 0, i % tiles))
    else:
        b_spec = pl.BlockSpec((tm, D_B), lambda i: (i, 0))
    return pl.pallas_call(
        functools.partial(_tail_kernel, b_transposed=b_transposed),
        grid=(t // tm,),
        in_specs=[pl.BlockSpec((tm, d), lambda i: (i, 0)), pl.BlockSpec((tm, D_A), lambda i: (i, 0)),
                  b_spec, resident((D_A + D_B, d)),
                  pl.BlockSpec((1, d), const), resident((d, d_ff)), resident((d, d_ff)),
                  resident((d_ff, d)), pl.BlockSpec((1, d), const)],
        out_specs=pl.BlockSpec((tm, d), lambda i: (i, 0)),
        out_shape=jax.ShapeDtypeStruct((t, d), F32),
        compiler_params=pltpu.CompilerParams(dimension_semantics=("arbitrary",),
                                             vmem_limit_bytes=VMEM_LIMIT),
        name="tail",
    )(x, a, bmix, wo16, g2, wg16, wu16, wd16, gf)


def kernel(x_prompt, x_sample, cache_k, cache_v, page_table, norm1_g, w_in, gmlp_ln_g, gmlp_ln_b, gmlp_ws, gmlp_bs, lambda_q1, lambda_k1, lambda_q2, lambda_k2, subln_g, rel_bias, w_out, norm2_g, w_gate, w_up, w_down, final_g):
    nb, seq, d = x_prompt.shape
    db, dec, _ = x_sample.shape
    row = lambda p: p.reshape(1, -1)

    lam = jnp.stack([lambda_q1, lambda_k1, lambda_q2, lambda_k2])
    proj_params = (row(norm1_g), w_in.astype(BF16), row(gmlp_ln_g), row(gmlp_ln_b))

    xp = x_prompt.reshape(nb * seq, d)
    bs_p = jnp.repeat(gmlp_bs.T, CH_A, axis=1)
    a_p, qt_p, kf_p, vf_p, kb_p, vt_p, wo16, wg16, wu16, wd16 = _in_proj(
        xp, *proj_params, gmlp_ws, bs_p, tm=1024, cs=CHUNK, seq=seq, cast=(w_out, w_gate, w_up, w_down))
    tail_params = (wo16, row(norm2_g), wg16, wu16, wd16, row(final_g))
    bt_p = _prompt_attention(rel_bias, qt_p, kb_p.reshape(nb, seq, D_QK), vt_p, lam,
                             subln_g.reshape(DV, 1), t=512, hp=2)
    y_p = _tail(xp, a_p, bt_p, *tail_params, tm=512)

    xs = x_sample.reshape(db * dec, d)
    pos = np.arange(db * dec)
    pick = jnp.asarray(pos[:, None] % dec == np.arange(dec)[None, :], F32)
    same = jnp.asarray(pos[:, None] // dec == pos[None, :] // dec)
    ws_s = jnp.where(same, jnp.einsum('rt,gts,cs->grc', pick, gmlp_ws[:, :dec, :dec], pick,
                                      precision=lax.Precision.HIGHEST), 0.0)
    bs_s = jnp.tile(jnp.repeat(gmlp_bs[:, :dec].T, CH_A, axis=1), (db, 1))
    a_s, q_s, kf_s, vf_s, gv_s = _in_proj(xs, *proj_params, ws_s, bs_s, tm=db * dec, cs=db * dec)
    b_s = _sample_attention(page_table, rel_bias, q_s.reshape(db, dec, D_QK),
                            kf_s.reshape(db, dec * H_B, LANES), vf_s.reshape(db, dec * H_B, LANES),
                            cache_k.reshape(-1, PAGE * H_B, DV), cache_v.reshape(-1, PAGE * H_B, DV),
                            lam, row(subln_g), pages=16)
    y_s = _tail(xs, a_s, b_s.reshape(db * dec, D_B), *tail_params, tm=db * dec)

    return (y_p.reshape(nb, seq, d), y_s.reshape(db, dec, d),
            kf_p.reshape(nb, seq, H_B, 2 * DH), vf_p.reshape(nb, seq, H_B, DV),
            kf_s.reshape(db, dec, H_B, 2 * DH), vf_s.reshape(db, dec, H_B, DV),
            gv_s.reshape(db, dec, D_A))
```
